```python
import math
import jax, jax.numpy as jnp
from jax import lax
import numpy as np

D_MODEL = 1024
BATCH = 8
SEQ = 4096
DEPTH = 1
DEC_BATCH = 128
DEC_SEQ = 4
PAST_LEN = 8192
PAGE_SIZE = 128

HEAD_DIM = 64
D_MIX = D_MODEL
FOX_HEADS = D_MIX // (2 * HEAD_DIM)
MOBA_HEADS = D_MIX // HEAD_DIM - FOX_HEADS
N_HEADS = FOX_HEADS + MOBA_HEADS
IN_COLS = 4 * D_MIX + FOX_HEADS
FOX_Q_BLOCK = 128
MOBA_BLOCK = 256
MOBA_TOPK = 3
MOBA_Q_CHUNK = 16
REL_BUCKETS = 32
REL_MAX_DIST = 128
RMS_EPS = 1e-6
NEG_INF = -1e30

kernel_name = "hymba_fox_moba_decode_step"


def rmsnorm(x, g):
    xf = x.astype(jnp.float32)
    y = xf * lax.rsqrt(jnp.mean(xf * xf, axis=-1, keepdims=True) + RMS_EPS)
    return (y * g.astype(jnp.float32)).astype(x.dtype)


def t5_bucket(dist):
    max_exact = REL_BUCKETS // 2
    d = jnp.maximum(dist, 0)
    df = jnp.maximum(d, 1).astype(jnp.float32)
    large = max_exact + (jnp.log(df / max_exact) / math.log(REL_MAX_DIST / max_exact)
                         * (REL_BUCKETS - max_exact)).astype(jnp.int32)
    large = jnp.minimum(large, REL_BUCKETS - 1)
    return jnp.where(d < max_exact, d, large)


def project(x, g_pre, w_in, b_f):
    n, s = x.shape[0], x.shape[1]
    h = rmsnorm(x, g_pre)
    proj = jnp.einsum('nsd,de->nse', h, w_in)
    q = proj[..., :D_MIX].reshape(n, s, N_HEADS, HEAD_DIM)
    k = proj[..., D_MIX:2 * D_MIX].reshape(n, s, N_HEADS, HEAD_DIM)
    v = proj[..., 2 * D_MIX:3 * D_MIX].reshape(n, s, N_HEADS, HEAD_DIM)
    gate = proj[..., 3 * D_MIX:4 * D_MIX]
    logf = jax.nn.log_sigmoid(proj[..., 4 * D_MIX:].astype(jnp.float32) + b_f.astype(jnp.float32))
    return q, k, v, gate, logf


def merge_out(x, o, gate, w_out, g_post):
    n, s = x.shape[0], x.shape[1]
    y = o.reshape(n, s, D_MIX) * jax.nn.silu(gate)
    y = jnp.einsum('nse,ed->nsd', y, w_out)
    return x + rmsnorm(y, g_post)


def fox_attend(q, k, v, cq, ck, q_pos, k_pos):
    scale = 1.0 / math.sqrt(q.shape[-1])
    logits = jnp.einsum('nqhd,nkhd->nhqk', q, k).astype(jnp.float32) * scale
    logits = logits + jnp.transpose(cq, (0, 2, 1))[..., None] - jnp.transpose(ck, (0, 2, 1))[:, :, None, :]
    causal = k_pos[None, :] <= q_pos[:, None]
    logits = jnp.where(causal[None, None], logits, NEG_INF)
    p = jax.nn.softmax(logits, axis=-1)
    return jnp.einsum('nhqk,nkhd->nqhd', p.astype(v.dtype), v)


def to_blocks(k):
    n, l, h, d = k.shape
    nb = -(-l // MOBA_BLOCK)
    k = jnp.pad(k, ((0, 0), (0, nb * MOBA_BLOCK - l), (0, 0), (0, 0)))
    return k.reshape(n, nb, MOBA_BLOCK, h, d).transpose(0, 3, 1, 2, 4)


def block_means(kb):
    return jnp.mean(kb.astype(jnp.float32), axis=3).astype(kb.dtype)


def moba_attend(q, q_pos, kb, vb, kbar, rel_bias):
    n, nq, h, d = q.shape
    nb, bs = kb.shape[2], kb.shape[3]
    scale = 1.0 / math.sqrt(d)
    qblk = q_pos // bs
    gate = jnp.einsum('nqhd,nhjd->nqhj', q, kbar).astype(jnp.float32)
    fully_past = jnp.arange(nb)[None, :] < qblk[:, None]
    gate = jnp.where(fully_past[None, :, None, :], gate, NEG_INF)
    if nb < MOBA_TOPK:
        gate = jnp.pad(gate, ((0, 0), (0, 0), (0, 0), (0, MOBA_TOPK - nb)), constant_values=NEG_INF)
    _, sel = lax.top_k(gate, MOBA_TOPK)
    valid = sel < qblk[None, :, None, None]
    own = jnp.broadcast_to(qblk[None, :, None, None], (n, nq, h, 1))
    blocks = jnp.concatenate([jnp.minimum(sel, nb - 1), own], axis=-1).astype(jnp.int32)
    n_idx = jnp.arange(n)[:, None, None, None]
    h_idx = jnp.arange(h)[None, None, :, None]
    kg = kb[n_idx, h_idx, blocks]
    vg = vb[n_idx, h_idx, blocks]
    key_pos = blocks[..., None] * bs + jnp.arange(bs)
    dist = q_pos[None, :, None, None, None] - key_pos
    bias = rel_bias[t5_bucket(dist), jnp.arange(h)[None, None, :, None, None]]
    slot_ok = jnp.concatenate([valid, jnp.ones_like(own, dtype=bool)], axis=-1)[..., None] & (dist >= 0)
    logits = jnp.einsum('nqhd,nqhjbd->nqhjb', q, kg).astype(jnp.float32) * scale + bias.astype(jnp.float32)
    logits = jnp.where(slot_ok, logits, NEG_INF)
    p = jax.nn.softmax(logits.reshape(n, nq, h, -1), axis=-1).reshape(logits.shape)
    return jnp.einsum('nqhjb,nqhjbd->nqhd', p.astype(vg.dtype), vg)


def setup_inputs(seed: int = 0) -> dict:
    key = jax.random.key(seed)
    ks = jax.random.split(key, 13)
    n_pages = PAST_LEN // PAGE_SIZE
    n_used = DEC_BATCH * n_pages
    n_pool = n_used + max(n_used // 4, 1)
    f32 = jnp.float32
    x_prompt = jax.random.normal(ks[0], (BATCH, SEQ, D_MODEL), f32)
    x_sample = jax.random.normal(ks[1], (DEC_BATCH, DEC_SEQ, D_MODEL), f32)
    cache_k = jax.random.normal(ks[2], (DEPTH, n_pool, PAGE_SIZE, N_HEADS, HEAD_DIM), f32)
    cache_v = jax.random.normal(ks[3], (DEPTH, n_pool, PAGE_SIZE, N_HEADS, HEAD_DIM), f32)
    cache_logf = jax.nn.log_sigmoid(3.0 + 0.5 * jax.random.normal(ks[4], (DEPTH, n_pool, PAGE_SIZE, FOX_HEADS), f32))
    page_table = jax.random.permutation(ks[5], n_pool)[:n_used].reshape(DEC_BATCH, n_pages).astype(jnp.int32)
    g_pre = 1.0 + 0.02 * jax.random.normal(ks[6], (DEPTH, D_MODEL), f32)
    w_in = jax.random.normal(ks[7], (DEPTH, D_MODEL, IN_COLS), f32) * D_MODEL ** -0.5
    b_f = jnp.linspace(1.0, 4.0, FOX_HEADS, dtype=f32)[None, :] + 0.1 * jax.random.normal(ks[8], (DEPTH, FOX_HEADS), f32)
    rel_bias = 0.2 * jax.random.normal(ks[9], (REL_BUCKETS, MOBA_HEADS), f32)
    w_out = jax.random.normal(ks[10], (DEPTH, D_MIX, D_MODEL), f32) * D_MIX ** -0.5
    g_post = 1.0 + 0.02 * jax.random.normal(ks[11], (DEPTH, D_MODEL), f32)
    return {"x_prompt": x_prompt, "x_sample": x_sample, "cache_k": cache_k, "cache_v": cache_v,
            "cache_logf": cache_logf, "page_table": page_table, "g_pre": g_pre, "w_in": w_in,
            "b_f": b_f, "rel_bias": rel_bias, "w_out": w_out, "g_post": g_post}


def reference(x_prompt, x_sample, cache_k, cache_v, cache_logf, page_table,
              g_pre, w_in, b_f, rel_bias, w_out, g_post):
    xp, xs = x_prompt, x_sample
    kp_l, vp_l, lp_l, ks_l, vs_l, ls_l = [], [], [], [], [], []
    for l in range(DEPTH):
        b, s = xp.shape[0], xp.shape[1]
        q, k, v, gate, logf = project(xp, g_pre[l], w_in[l], b_f[l])
        pos = jnp.arange(s, dtype=jnp.int32)
        qf, kf, vf = q[:, :, :FOX_HEADS], k[:, :, :FOX_HEADS], v[:, :, :FOX_HEADS]
        cum = lax.cumsum(logf, axis=1)
        nqb = s // FOX_Q_BLOCK
        q_blocks = qf.reshape(b, nqb, FOX_Q_BLOCK, FOX_HEADS, HEAD_DIM).transpose(1, 0, 2, 3, 4)
        c_blocks = cum.reshape(b, nqb, FOX_Q_BLOCK, FOX_HEADS).transpose(1, 0, 2, 3)
        p_blocks = pos.reshape(nqb, FOX_Q_BLOCK)
        o_fox = lax.map(lambda a: fox_attend(a[0], kf, vf, a[1], cum, a[2], pos), (q_blocks, c_blocks, p_blocks))
        o_fox = o_fox.transpose(1, 0, 2, 3, 4).reshape(b, s, FOX_HEADS, HEAD_DIM)
        kb = to_blocks(k[:, :, FOX_HEADS:])
        vb = to_blocks(v[:, :, FOX_HEADS:])
        kbar = block_means(kb)
        nqc = s // MOBA_Q_CHUNK
        q_chunks = q[:, :, FOX_HEADS:].reshape(b, nqc, MOBA_Q_CHUNK, MOBA_HEADS, HEAD_DIM).transpose(1, 0, 2, 3, 4)
        p_chunks = pos.reshape(nqc, MOBA_Q_CHUNK)
        o_moba = lax.map(lambda a: moba_attend(a[0], a[1], kb, vb, kbar, rel_bias), (q_chunks, p_chunks))
        o_moba = o_moba.transpose(1, 0, 2, 3, 4).reshape(b, s, MOBA_HEADS, HEAD_DIM)
        xp = merge_out(xp, jnp.concatenate([o_fox, o_moba], axis=2), gate, w_out[l], g_post[l])
        kp_l.append(k)
        vp_l.append(v)
        lp_l.append(logf)

        qs, ks_, vs_, gate_s, logf_s = project(xs, g_pre[l], w_in[l], b_f[l])
        ck_l, cv_l, cl_l = cache_k[l], cache_v[l], cache_logf[l]

        def seq_step(args):
            pt, qn, kn, vn, lfn = args
            kpast = ck_l[pt].reshape(-1, N_HEADS, HEAD_DIM)
            vpast = cv_l[pt].reshape(-1, N_HEADS, HEAD_DIM)
            lpast = cl_l[pt].reshape(-1, FOX_HEADS).astype(jnp.float32)
            past = kpast.shape[0]
            k_all = jnp.concatenate([kpast, kn.astype(kpast.dtype)], axis=0)
            v_all = jnp.concatenate([vpast, vn.astype(vpast.dtype)], axis=0)
            c_all = lax.cumsum(jnp.concatenate([lpast, lfn], axis=0), axis=0)
            q_pos = past + jnp.arange(qn.shape[0], dtype=jnp.int32)
            k_pos = jnp.arange(k_all.shape[0], dtype=jnp.int32)
            of = fox_attend(qn[None, :, :FOX_HEADS], k_all[None, :, :FOX_HEADS], v_all[None, :, :FOX_HEADS],
                            c_all[None, past:], c_all[None], q_pos, k_pos)[0]
            kbs = to_blocks(k_all[None, :, FOX_HEADS:])
            vbs = to_blocks(v_all[None, :, FOX_HEADS:])
            om = moba_attend(qn[None, :, FOX_HEADS:], q_pos, kbs, vbs, block_means(kbs), rel_bias)[0]
            return jnp.concatenate([of, om.astype(of.dtype)], axis=1)

        o_s = lax.map(seq_step, (page_table, qs, ks_, vs_, logf_s))
        xs = merge_out(xs, o_s.astype(xs.dtype), gate_s, w_out[l], g_post[l])
        ks_l.append(ks_)
        vs_l.append(vs_)
        ls_l.append(logf_s)
    return (xp, xs, jnp.stack(kp_l), jnp.stack(vp_l), jnp.stack(lp_l),
            jnp.stack(ks_l), jnp.stack(vs_l), jnp.stack(ls_l))
```

```python
import functools
import math

import jax
import jax.numpy as jnp
from jax import lax
from jax.experimental import pallas as pl
from jax.experimental.pallas import tpu as pltpu

HEAD_DIM = 64
LANES = 128
MOBA_BLOCK = 256
MOBA_TOPK = 3
REL_BUCKETS = 32
REL_MAX_DIST = 128
RMS_EPS = 1e-6
NEG = -1e30
PICKED = -3e38
VMEM_LIMIT = 56 * 1024 * 1024

F32 = jnp.float32
BF16 = jnp.bfloat16
NT = (((1,), (1,)), ((), ()))


def _dot(a, b):
    return jnp.dot(a, b, preferred_element_type=F32)


def _dot_nt(a, b):
    return lax.dot_general(a, b, NT, preferred_element_type=F32)


def _split3(x):
    hi = x.astype(BF16)
    r1 = x - hi.astype(F32)
    mid = r1.astype(BF16)
    lo = (r1 - mid.astype(F32)).astype(BF16)
    return hi, mid, lo


def _dot_exact(x, m):
    hi, mid, lo = _split3(x)
    return _dot(hi, m) + _dot(mid, m) + _dot(lo, m)


def _dot_nt_exact(a, x):
    hi, mid, lo = _split3(x)
    return _dot_nt(a, hi) + _dot_nt(a, mid) + _dot_nt(a, lo)


def _bucket(dist):
    max_exact = REL_BUCKETS // 2
    d = jnp.maximum(dist, 0)
    df = jnp.maximum(d, 1).astype(F32)
    large = max_exact + (jnp.log(df / max_exact) / math.log(REL_MAX_DIST / max_exact)
                         * (REL_BUCKETS - max_exact)).astype(jnp.int32)
    large = jnp.minimum(large, REL_BUCKETS - 1)
    return jnp.where(d < max_exact, d, large)


def _top_blocks(gate, blk, n_past):
    blkf = blk.astype(F32)
    sel = jnp.zeros(gate.shape, jnp.bool_)
    for _ in range(MOBA_TOPK):
        mx = jnp.max(gate, axis=1, keepdims=True)
        idx = jnp.min(jnp.where(gate == mx, blkf, float(LANES)), axis=1, keepdims=True)
        pick = blkf == idx
        sel = sel | (pick & (blk < n_past))
        gate = jnp.where(pick, PICKED, gate)
    return sel


def _proj_kernel(x_ref, gpre_ref, w_ref, wft_ref, bf_ref, tri_ref,
                 q_ref, k_ref, v_ref, g_ref, k32_ref, v32_ref, lf_ref, cum_ref, kbar_ref,
                 carry_ref, *, tiles_per_seq, tm, d):
    t = pl.program_id(0) % tiles_per_seq
    x = x_ref[...]
    h = x * lax.rsqrt(jnp.mean(x * x, axis=-1, keepdims=True) + RMS_EPS) * gpre_ref[...]
    hb = h.astype(BF16)

    q_ref[...] = _dot(hb, w_ref[:, 0:d]).astype(BF16)
    kf = _dot(hb, w_ref[:, d:2 * d])
    k32_ref[...] = kf
    k_ref[...] = kf.astype(BF16)
    vf = _dot(hb, w_ref[:, 2 * d:3 * d])
    v32_ref[...] = vf
    v_ref[...] = vf.astype(BF16)
    gf = _dot(hb, w_ref[:, 3 * d:4 * d])
    g_ref[...] = (gf * jax.nn.sigmoid(gf)).astype(BF16)

    u = _dot_nt(wft_ref[...], hb) + bf_ref[:, 0:1]
    lf = jnp.minimum(u, 0.0) - jnp.log1p(jnp.exp(-jnp.abs(u)))
    lf_ref[...] = lf[0:8]

    @pl.when(t == 0)
    def _():
        carry_ref[...] = jnp.zeros_like(carry_ref)
        kbar_ref[...] = jnp.zeros_like(kbar_ref)

    cum = carry_ref[:, 0:1] + _dot_exact(lf, tri_ref[...])
    cum_ref[...] = cum[0:8]
    carry_ref[...] = jnp.broadcast_to(cum[:, tm - 1:tm], carry_ref.shape)

    for bb in range(tm // MOBA_BLOCK):
        ks = jnp.sum(kf[bb * MOBA_BLOCK:(bb + 1) * MOBA_BLOCK, d // 2:], axis=0, keepdims=True)
        kbar_ref[0, pl.ds(t * (tm // MOBA_BLOCK) + bb, 1), :] = ks * (1.0 / MOBA_BLOCK)


def _proj(x2, gpre, wq, wft, bfb, tri, *, tm, tiles_per_seq):
    rows, d = x2.shape
    n_seq = rows // (tm * tiles_per_seq)
    row_blk = lambda dt: (pl.BlockSpec((tm, d), lambda i: (i, 0)), jax.ShapeDtypeStruct((rows, d), dt))
    const = lambda shape: pl.BlockSpec(shape, lambda i: (0,) * len(shape))
    outs = [row_blk(BF16)] * 4 + [row_blk(F32)] * 2
    outs += [(pl.BlockSpec((8, tm), lambda i: (0, i)), jax.ShapeDtypeStruct((8, rows), F32))] * 2
    outs += [(pl.BlockSpec((1, LANES, d // 2), lambda i: (i // tiles_per_seq, 0, 0)),
              jax.ShapeDtypeStruct((n_seq, LANES, d // 2), F32))]
    return pl.pallas_call(
        functools.partial(_proj_kernel, tiles_per_seq=tiles_per_seq, tm=tm, d=d),
        grid=(rows // tm,),
        in_specs=[pl.BlockSpec((tm, d), lambda i: (i, 0)), const((1, d)), const((d, 4 * d)),
                  const((16, d)), const((16, LANES)), const((tm, tm))],
        out_specs=[o[0] for o in outs],
        out_shape=[o[1] for o in outs],
        scratch_shapes=[pltpu.VMEM((16, LANES), F32)],
        compiler_params=pltpu.CompilerParams(dimension_semantics=("arbitrary",),
                                             vmem_limit_bytes=VMEM_LIMIT),
        name="proj",
    )(x2, gpre, wq, wft, bfb, tri)


def _prompt_bias_kernel(rb_ref, tbl_ref):
    t = pl.program_id(0)
    h = pl.program_id(1)
    r = lax.broadcasted_iota(jnp.int32, (MOBA_BLOCK, MOBA_BLOCK), 0)
    c = lax.broadcasted_iota(jnp.int32, (MOBA_BLOCK, MOBA_BLOCK), 1)
    dist = r - c + t * MOBA_BLOCK
    bucket = _bucket(dist)
    bias = jnp.zeros((MOBA_BLOCK, MOBA_BLOCK), F32)
    for b in range(REL_BUCKETS):
        bias = jnp.where(bucket == b, rb_ref[b, h], bias)
    tbl_ref[0, 0] = jnp.where(dist >= 0, bias, NEG)


def _prompt_bias(rel_bias):
    n_heads = rel_bias.shape[1]
    return pl.pallas_call(
        _prompt_bias_kernel,
        grid=(2, n_heads),
        in_specs=[pl.BlockSpec(memory_space=pltpu.SMEM)],
        out_specs=pl.BlockSpec((1, 1, MOBA_BLOCK, MOBA_BLOCK), lambda t, h: (t, h, 0, 0)),
        out_shape=jax.ShapeDtypeStruct((2, n_heads, MOBA_BLOCK, MOBA_BLOCK), F32),
        name="prompt_bias",
    )(rel_bias)


def _decode_bias_kernel(rb_ref, decb_ref, decnew_ref, *, dq, n_heads):
    rows = 2 * dq * n_heads
    half = rows // 2

    def lookup(bucket, hrow):
        out = jnp.zeros(bucket.shape, F32)
        for hh in range(n_heads):
            for b in range(REL_BUCKETS):
                out = jnp.where((bucket == b) & (hrow == hh), rb_ref[b, hh], out)
        return out

    r = lax.broadcasted_iota(jnp.int32, (rows, MOBA_BLOCK), 0)
    c = lax.broadcasted_iota(jnp.int32, (rows, MOBA_BLOCK), 1)
    is_moba = r >= half
    qq = (r % half) // n_heads
    hrow = r % n_heads
    far = lookup(jnp.full(r.shape, REL_BUCKETS - 1, jnp.int32), hrow)
    near = lookup(_bucket(MOBA_BLOCK + qq - c), hrow)
    decb_ref[0] = jnp.where(is_moba, far, 0.0)
    decb_ref[1] = jnp.where(is_moba, near, 0.0)

    r = lax.broadcasted_iota(jnp.int32, (rows, LANES), 0)
    c = lax.broadcasted_iota(jnp.int32, (rows, LANES), 1)
    qq = (r % half) // n_heads
    dist = qq - c
    own = lookup(_bucket(dist), r % n_heads)
    decnew_ref[...] = jnp.where(dist >= 0, jnp.where(r >= half, own, 0.0), NEG)


def _decode_bias(rel_bias, dq):
    n_heads = rel_bias.shape[1]
    rows = 2 * dq * n_heads
    return pl.pallas_call(
        functools.partial(_decode_bias_kernel, dq=dq, n_heads=n_heads),
        in_specs=[pl.BlockSpec(memory_space=pltpu.SMEM)],
        out_shape=[jax.ShapeDtypeStruct((2, rows, MOBA_BLOCK), F32),
                   jax.ShapeDtypeStruct((rows, LANES), F32)],
        name="decode_bias",
    )(rel_bias)


def _stack_heads(q2):
    lane = lax.broadcasted_iota(jnp.int32, q2.shape, 1)
    qs = q2 * (1.0 / math.sqrt(HEAD_DIM))
    zero = jnp.zeros_like(qs)
    return jnp.concatenate([jnp.where(lane < HEAD_DIM, qs, zero),
                            jnp.where(lane >= HEAD_DIM, qs, zero)], axis=0).astype(BF16)


def _online_step(carry, s, vt):
    m, l, acc = carry
    m_new = jnp.maximum(m, jnp.max(s, axis=1, keepdims=True))
    alpha = jnp.exp(m - m_new)
    p = jnp.exp(s - m_new)
    l = alpha * l + jnp.sum(p, axis=1, keepdims=True)
    acc = alpha * acc + _dot(p.astype(BF16), vt)
    return m_new, l, acc


def _finish_heads(carry, tq):
    _, l, acc = carry
    o = acc / l
    lane = lax.broadcasted_iota(jnp.int32, (tq, LANES), 1)
    return jnp.where(lane < HEAD_DIM, o[:tq], o[tq:])


def _init_carry(tq):
    return (jnp.full((2 * tq, 1), NEG, F32), jnp.zeros((2 * tq, 1), F32), jnp.zeros((2 * tq, LANES), F32))


def _fox_kernel(q_ref, k_ref, v_ref, cum_ref, o_ref, *, tq):
    hp = pl.program_id(1)
    i = pl.program_id(2)
    qs = _stack_heads(q_ref[...])

    def tile(j, carry, diag):
        off = pl.multiple_of(j * tq, tq)
        kt = k_ref[pl.ds(off, tq), :]
        vt = v_ref[pl.ds(off, tq), :]
        s = _dot_nt(qs, kt)
        ck0 = cum_ref[pl.ds(2 * hp, 1), pl.ds(off, tq)]
        ck1 = cum_ref[pl.ds(2 * hp + 1, 1), pl.ds(off, tq)]
        s = jnp.concatenate([s[:tq] - ck0, s[tq:] - ck1], axis=0)
        if diag:
            r = lax.broadcasted_iota(jnp.int32, (2 * tq, tq), 0) % tq
            c = lax.broadcasted_iota(jnp.int32, (2 * tq, tq), 1)
            s = jnp.where(c <= r, s, NEG)
        return _online_step(carry, s, vt)

    carry = lax.fori_loop(0, i, lambda j, cr: tile(j, cr, False), _init_carry(tq))
    carry = tile(i, carry, True)
    o_ref[...] = _finish_heads(carry, tq).astype(o_ref.dtype)


def _fox(q, k, v, cum, *, batch, seq, tq):
    rows, d = q.shape
    nq = seq // tq
    n_pairs = d // 2 // LANES
    return pl.pallas_call(
        functools.partial(_fox_kernel, tq=tq),
        grid=(batch, n_pairs, nq),
        in_specs=[pl.BlockSpec((tq, LANES), lambda b, hp, i: (b * nq + i, hp)),
                  pl.BlockSpec((seq, LANES), lambda b, hp, i: (b, hp)),
                  pl.BlockSpec((seq, LANES), lambda b, hp, i: (b, hp)),
                  pl.BlockSpec((8, seq), lambda b, hp, i: (0, b))],
        out_specs=pl.BlockSpec((tq, LANES), lambda b, hp, i: (b * nq + i, hp)),
        out_shape=jax.ShapeDtypeStruct((rows, d // 2), BF16),
        compiler_params=pltpu.CompilerParams(dimension_semantics=("arbitrary",) * 3,
                                             vmem_limit_bytes=VMEM_LIMIT),
        name="fox",
    )(q, k, v, cum)


def _moba_kernel(rb_ref, q_ref, k_ref, v_ref, kbar_ref, tbl_ref, o_ref, *, tq):
    hp = pl.program_id(1)
    i = pl.program_id(2)
    qs = _stack_heads(q_ref[...])

    gate = _dot_nt_exact(qs, kbar_ref[0])
    blk = lax.broadcasted_iota(jnp.int32, (2 * tq, LANES), 1)
    sel = _top_blocks(jnp.where(blk < i, gate, NEG), blk, i)
    selb = jnp.where(sel, 0.0, NEG)

    def sel_col(j):
        return jnp.sum(jnp.where(blk == j, selb, 0.0), axis=1, keepdims=True)

    far = jnp.concatenate([jnp.full((tq, 1), rb_ref[REL_BUCKETS - 1, 2 * hp], F32),
                           jnp.full((tq, 1), rb_ref[REL_BUCKETS - 1, 2 * hp + 1], F32)], axis=0)

    def qk(j):
        off = pl.multiple_of(j * tq, tq)
        return _dot_nt(qs, k_ref[pl.ds(off, tq), :]), v_ref[pl.ds(off, tq), :]

    def table(t):
        return jnp.concatenate([tbl_ref[t, 0], tbl_ref[t, 1]], axis=0)

    def far_tile(j, carry):
        s, vt = qk(j)
        return _online_step(carry, s + (sel_col(j) + far), vt)

    def prev_tile(carry):
        s, vt = qk(i - 1)
        return _online_step(carry, s + table(1) + sel_col(i - 1), vt)

    carry = lax.fori_loop(0, jnp.maximum(i - 1, 0), far_tile, _init_carry(tq))
    carry = lax.cond(i >= 1, prev_tile, lambda cr: cr, carry)
    s, vt = qk(i)
    carry = _online_step(carry, s + table(0), vt)
    o_ref[...] = _finish_heads(carry, tq).astype(o_ref.dtype)


def _moba(rel_bias, q, k, v, kbar, tbl, *, batch, seq):
    rows, d = q.shape
    tq = MOBA_BLOCK
    nq = seq // tq
    n_pairs = d // 2 // LANES
    return pl.pallas_call(
        functools.partial(_moba_kernel, tq=tq),
        grid=(batch, n_pairs, nq),
        in_specs=[pl.BlockSpec(memory_space=pltpu.SMEM),
                  pl.BlockSpec((tq, LANES), lambda b, hp, i: (b * nq + i, n_pairs + hp)),
                  pl.BlockSpec((seq, LANES), lambda b, hp, i: (b, n_pairs + hp)),
                  pl.BlockSpec((seq, LANES), lambda b, hp, i: (b, n_pairs + hp)),
                  pl.BlockSpec((1, LANES, LANES), lambda b, hp, i: (b, 0, hp)),
                  pl.BlockSpec((2, 2, tq, tq), lambda b, hp, i: (0, hp, 0, 0))],
        out_specs=pl.BlockSpec((tq, LANES), lambda b, hp, i: (b * nq + i, hp)),
        out_shape=jax.ShapeDtypeStruct((rows, d // 2), BF16),
        compiler_params=pltpu.CompilerParams(dimension_semantics=("arbitrary",) * 3,
                                             vmem_limit_bytes=VMEM_LIMIT),
        name="moba",
    )(rel_bias, q, k, v, kbar, tbl)


def _merge_kernel(x_ref, oa_ref, ob_ref, g_ref, w_ref, gpost_ref, out_ref, *, d):
    g = g_ref[...].astype(F32)
    ya = (oa_ref[...].astype(F32) * g[:, :d // 2]).astype(BF16)
    yb = (ob_ref[...].astype(F32) * g[:, d // 2:]).astype(BF16)
    y = _dot(ya, w_ref[0:d // 2, :]) + _dot(yb, w_ref[d // 2:, :])
    n = y * lax.rsqrt(jnp.mean(y * y, axis=-1, keepdims=True) + RMS_EPS) * gpost_ref[...]
    out_ref[...] = x_ref[...] + n


def _merge(x2, oa, ob, ob_col, g, wo, gpost, *, tm):
    rows, d = x2.shape
    return pl.pallas_call(
        functools.partial(_merge_kernel, d=d),
        grid=(rows // tm,),
        in_specs=[pl.BlockSpec((tm, d), lambda i: (i, 0)),
                  pl.BlockSpec((tm, d // 2), lambda i: (i, 0)),
                  pl.BlockSpec((tm, d // 2), lambda i: (i, ob_col)),
                  pl.BlockSpec((tm, d), lambda i: (i, 0)),
                  pl.BlockSpec((d, d), lambda i: (0, 0)),
                  pl.BlockSpec((1, d), lambda i: (0, 0))],
        out_specs=pl.BlockSpec((tm, d), lambda i: (i, 0)),
        out_shape=jax.ShapeDtypeStruct((rows, d), F32),
        compiler_params=pltpu.CompilerParams(dimension_semantics=("arbitrary",),
                                             vmem_limit_bytes=VMEM_LIMIT),
        name="merge",
    )(x2, oa, ob, g, wo, gpost)


def _decode_kernel(pt_ref, q_ref, kn_ref, vn_ref, lfn_ref, ka_ref, kb_ref, va_ref, vb_ref,
                   la_ref, lb_ref, decb_ref, decnew_ref, tri_ref, o_ref,
                   qbd_ref, m_ref, l_ref, acc_ref, carry_ref, pm_ref, pls_ref, po_ref, kbar_ref,
                   *, nblk, dq, d):
    del pt_ref
    j = pl.program_id(1)
    half = d // 2
    n_heads = half // HEAD_DIM
    hr = dq * n_heads

    def bcast(x, ref):
        return jnp.broadcast_to(x, ref.shape)

    @pl.when(j == 0)
    def _init():
        q4 = q_ref[0].astype(F32) * (1.0 / math.sqrt(HEAD_DIM))
        rows = [jnp.broadcast_to(q4[qq:qq + 1], (n_heads, d)) for qq in range(dq)]
        qrep = jnp.concatenate(rows + rows, axis=0)
        r = lax.broadcasted_iota(jnp.int32, (2 * hr, d), 0)
        c = lax.broadcasted_iota(jnp.int32, (2 * hr, d), 1)
        head = (r // hr) * n_heads + r % n_heads
        qbd_ref[...] = jnp.where(c // HEAD_DIM == head, qrep, 0.0).astype(BF16)
        m_ref[...] = jnp.full(m_ref.shape, NEG, F32)
        l_ref[...] = jnp.zeros_like(l_ref)
        acc_ref[...] = jnp.zeros_like(acc_ref)
        carry_ref[...] = jnp.zeros_like(carry_ref)
        pm_ref[...] = jnp.full(pm_ref.shape, NEG, F32)
        pls_ref[...] = jnp.zeros_like(pls_ref)
        kbar_ref[...] = jnp.zeros_like(kbar_ref)

    qf = qbd_ref[0:hr, 0:half]
    qm = qbd_ref[hr:, half:]

    def fox_bias(cum):
        return jnp.concatenate([cum[0:n_heads]] * dq, axis=0)

    def partial_softmax(s, v16):
        mb = jnp.max(s, axis=1, keepdims=True)
        p = jnp.exp(s - mb)
        return mb, jnp.sum(p, axis=1, keepdims=True), _dot(p.astype(BF16), v16)

    def fox_merge(mb, lb, ob):
        m_old = m_ref[:, 0:1]
        m_new = jnp.maximum(m_old, mb)
        a = jnp.exp(m_old - m_new)
        b = jnp.exp(mb - m_new)
        l_ref[...] = bcast(a * l_ref[:, 0:1] + b * lb, l_ref)
        acc_ref[...] = a * acc_ref[...] + b * ob
        m_ref[...] = bcast(m_new, m_ref)

    k_blk = jnp.concatenate([ka_ref[0], kb_ref[0]], axis=0)
    v_blk = jnp.concatenate([va_ref[0], vb_ref[0]], axis=0)
    k16 = k_blk.astype(BF16)
    v16 = v_blk.astype(BF16)
    kbar_ref[pl.ds(j, 1), :] = jnp.sum(k_blk[:, half:], axis=0, keepdims=True) * (1.0 / MOBA_BLOCK)

    lf = jnp.concatenate([la_ref[0], lb_ref[0]], axis=1)
    lf = jnp.concatenate([lf, jnp.zeros_like(lf)], axis=0)
    cum = carry_ref[:, 0:1] + _dot_exact(lf, tri_ref[...])
    carry_ref[...] = bcast(cum[:, MOBA_BLOCK - 1:MOBA_BLOCK], carry_ref)

    tb = decb_ref[jnp.where(j == nblk - 1, 1, 0)]
    s_f = _dot_nt(qf, k16[:, 0:half]) - fox_bias(cum)
    s_m = _dot_nt(qm, k16[:, half:]) + tb[hr:]
    fox_merge(*partial_softmax(s_f, v16[:, 0:half]))
    mb, lb, ob = partial_softmax(s_m, v16[:, half:])
    lane = lax.broadcasted_iota(jnp.int32, pm_ref.shape, 1)
    pm_ref[...] = jnp.where(lane == j, mb, pm_ref[...])
    pls_ref[...] = jnp.where(lane == j, lb, pls_ref[...])
    po_ref[j] = ob

    @pl.when(j == nblk - 1)
    def _finish():
        pad = jnp.zeros((LANES - kn_ref.shape[1], d), F32)
        kn = jnp.concatenate([kn_ref[0], pad], axis=0).astype(BF16)
        vn = jnp.concatenate([vn_ref[0], pad], axis=0).astype(BF16)
        dn = decnew_ref[...]

        def heads_to_cols(o):
            r = lax.broadcasted_iota(jnp.int32, o.shape, 0)
            c = lax.broadcasted_iota(jnp.int32, o.shape, 1)
            o = jnp.where(c // HEAD_DIM == r % n_heads, o, 0.0)
            return jnp.concatenate([jnp.sum(o[qq * n_heads:(qq + 1) * n_heads], axis=0, keepdims=True)
                                    for qq in range(dq)], axis=0)

        lfn = lfn_ref[0]
        lfn = jnp.concatenate([lfn, jnp.zeros_like(lfn)], axis=0)
        cn = carry_ref[:, 0:1] + _dot_exact(lfn, tri_ref[0:LANES, 0:LANES])
        s_n = _dot_nt(qf, kn[:, 0:half]) - fox_bias(cn) + dn[0:hr]
        fox_merge(*partial_softmax(s_n, vn[:, 0:half]))
        o_fox = heads_to_cols(acc_ref[...] / l_ref[:, 0:1])

        blk = lax.broadcasted_iota(jnp.int32, (hr, LANES), 1)
        gate = _dot_nt_exact(qm, kbar_ref[...])
        gate = jnp.where(blk < nblk, gate, jnp.where(blk == nblk, NEG, PICKED))
        sel = _top_blocks(gate, blk, nblk)
        m_n, l_n, o_n = partial_softmax(_dot_nt(qm, kn[:, half:]) + dn[hr:], vn[:, half:])
        pm_sel = jnp.where(sel, pm_ref[...], NEG)
        m_tot = jnp.maximum(jnp.max(pm_sel, axis=1, keepdims=True), m_n)
        w = jnp.exp(pm_sel - m_tot)
        w_n = jnp.exp(m_n - m_tot)
        l_tot = jnp.sum(w * pls_ref[...], axis=1, keepdims=True) + w_n * l_n
        o_tot = w_n * o_n
        for jj in range(nblk):
            o_tot = o_tot + w[:, jj:jj + 1] * po_ref[jj]
        o_moba = heads_to_cols(o_tot / l_tot)

        o_ref[0] = jnp.concatenate([o_fox, o_moba], axis=1)


def _decode(page_table, q3, kn3, vn3, lfn3, ck, cv, clt, decb, decnew, tri, *, page):
    n, dq, d = q3.shape
    n_pages = page_table.shape[1]
    ppb = MOBA_BLOCK // page
    nblk = n_pages // ppb
    half = d // 2
    hr = dq * (half // HEAD_DIM)
    kv_spec = lambda o: pl.BlockSpec((1, page, d), lambda s, j, pt: (pt[s, ppb * j + o], 0, 0))
    lf_spec = lambda o: pl.BlockSpec((1, 8, page), lambda s, j, pt: (pt[s, ppb * j + o], 0, 0))
    seq_spec = lambda a: pl.BlockSpec((1,) + a.shape[1:], lambda s, j, pt: (s, 0, 0))
    const = lambda a: pl.BlockSpec(a.shape, lambda s, j, pt: (0,) * a.ndim)
    grid_spec = pltpu.PrefetchScalarGridSpec(
        num_scalar_prefetch=1,
        grid=(n, nblk),
        in_specs=[seq_spec(q3), seq_spec(kn3), seq_spec(vn3), seq_spec(lfn3),
                  kv_spec(0), kv_spec(1), kv_spec(0), kv_spec(1), lf_spec(0), lf_spec(1),
                  const(decb), const(decnew), const(tri)],
        out_specs=pl.BlockSpec((1, dq, d), lambda s, j, pt: (s, 0, 0)),
        scratch_shapes=[pltpu.VMEM((2 * hr, d), BF16),
                        pltpu.VMEM((hr, LANES), F32),
                        pltpu.VMEM((hr, LANES), F32),
                        pltpu.VMEM((hr, half), F32),
                        pltpu.VMEM((16, LANES), F32),
                        pltpu.VMEM((hr, LANES), F32),
                        pltpu.VMEM((hr, LANES), F32),
                        pltpu.VMEM((nblk, hr, half), F32),
                        pltpu.VMEM((LANES, half), F32)])
    return pl.pallas_call(
        functools.partial(_decode_kernel, nblk=nblk, dq=dq, d=d),
        grid_spec=grid_spec,
        out_shape=jax.ShapeDtypeStruct((n, dq, d), F32),
        compiler_params=pltpu.CompilerParams(dimension_semantics=("arbitrary", "arbitrary"),
                                             vmem_limit_bytes=VMEM_LIMIT),
        name="decode",
    )(page_table, q3, kn3, vn3, lfn3, ck, ck, cv, cv, clt, clt, decb, decnew, tri)


def _tri(n):
    r = lax.broadcasted_iota(jnp.int32, (n, n), 0)
    c = lax.broadcasted_iota(jnp.int32, (n, n), 1)
    return (r <= c).astype(BF16)


def kernel(x_prompt, x_sample, cache_k, cache_v, cache_logf, page_table, g_pre, w_in, b_f, rel_bias, w_out, g_post):
    batch, seq, d = x_prompt.shape
    n_dec, dq, _ = x_sample.shape
    depth, n_pool, page = cache_k.shape[0], cache_k.shape[1], cache_k.shape[2]
    n_heads = d // HEAD_DIM
    fox_heads = n_heads // 2
    assert d == 2 * fox_heads * HEAD_DIM and fox_heads == 8 and rel_bias.shape == (REL_BUCKETS, fox_heads)
    assert seq % 512 == 0 and MOBA_BLOCK % page == 0 and (page_table.shape[1] * page) % MOBA_BLOCK == 0
    assert dq <= 8 and (n_dec * dq) % 8 == 0 and page_table.shape[1] * page // MOBA_BLOCK < LANES

    tm = 512
    tm_s = n_dec * dq if n_dec * dq <= 512 else 512
    assert (n_dec * dq) % tm_s == 0
    tri_p, tri_s, tri_d = _tri(tm), _tri(tm_s), _tri(MOBA_BLOCK)
    tbl = _prompt_bias(rel_bias)
    decb, decnew = _decode_bias(rel_bias, dq)

    xp = x_prompt.reshape(batch * seq, d)
    xs = x_sample.reshape(n_dec * dq, d)
    kp_l, vp_l, lp_l, ks_l, vs_l, ls_l = [], [], [], [], [], []
    for l in range(depth):
        wq = w_in[l][:, :4 * d].astype(BF16)
        wft = jnp.pad(w_in[l][:, 4 * d:].T, ((0, 16 - fox_heads), (0, 0))).astype(BF16)
        bfb = jnp.broadcast_to(jnp.pad(b_f[l], (0, 16 - fox_heads))[:, None], (16, LANES)).astype(F32)
        gpre = g_pre[l][None, :]
        gpost = g_post[l][None, :]
        wo = w_out[l].astype(BF16)

        q, k, v, g, k32, v32, lft, cum, kbar = _proj(xp, gpre, wq, wft, bfb, tri_p,
                                                     tm=tm, tiles_per_seq=seq // tm)
        o_fox = _fox(q, k, v, cum, batch=batch, seq=seq, tq=MOBA_BLOCK)
        o_moba = _moba(rel_bias, q, k, v, kbar, tbl, batch=batch, seq=seq)
        kp_l.append(k32.reshape(batch, seq, n_heads, HEAD_DIM))
        vp_l.append(v32.reshape(batch, seq, n_heads, HEAD_DIM))
        lp_l.append(lft.T.reshape(batch, seq, fox_heads))

        qs, _, _, gs, k32s, v32s, lfts, _, _ = _proj(xs, gpre, wq, wft, bfb, tri_s,
                                                     tm=tm_s, tiles_per_seq=1)
        pad_rows = lambda a: jnp.pad(a.reshape(n_dec, dq, d), ((0, 0), (0, 8 - dq), (0, 0)))
        lfn3 = jnp.pad(lfts.reshape(8, n_dec, dq).transpose(1, 0, 2), ((0, 0), (0, 0), (0, LANES - dq)))
        o_s = _decode(page_table, qs.reshape(n_dec, dq, d), pad_rows(k32s), pad_rows(v32s), lfn3,
                      cache_k[l].reshape(n_pool, page, d), cache_v[l].reshape(n_pool, page, d),
                      cache_logf[l].transpose(0, 2, 1), decb, decnew, tri_d, page=page)
        o_s = o_s.reshape(n_dec * dq, d)
        ks_l.append(k32s.reshape(n_dec, dq, n_heads, HEAD_DIM))
        vs_l.append(v32s.reshape(n_dec, dq, n_heads, HEAD_DIM))
        ls_l.append(lfts.T.reshape(n_dec, dq, fox_heads))

        xp = _merge(xp, o_fox, o_moba, 0, g, wo, gpost, tm=tm)
        xs = _merge(xs, o_s, o_s, 1, gs, wo, gpost, tm=tm_s)

    return (xp.reshape(batch, seq, d), xs.reshape(n_dec, dq, d), jnp.stack(kp_l), jnp.stack(vp_l),
            jnp.stack(lp_l), jnp.stack(ks_l), jnp.stack(vs_l), jnp.stack(ls_l))
```

```python
import functools
import math

import jax
import jax.numpy as jnp
from jax import lax
from jax.experimental import pallas as pl
from jax.experimental.pallas import tpu as pltpu

HEAD_DIM = 64
LANES = 128
MOBA_BLOCK = 256
MOBA_TOPK = 3
REL_BUCKETS = 32
REL_MAX_DIST = 128
RMS_EPS = 1e-6
NEG = -1e30
PICKED = -3e38
VMEM_LIMIT = 56 * 1024 * 1024
DECODE_BLOCKS_PER_STEP = 2

F32 = jnp.float32
BF16 = jnp.bfloat16
NT = (((1,), (1,)), ((), ()))


def _dot(a, b):
    return jnp.dot(a, b, preferred_element_type=F32)


def _dot_nt(a, b):
    return lax.dot_general(a, b, NT, preferred_element_type=F32)


def _split3(x):
    hi = x.astype(BF16)
    r1 = x - hi.astype(F32)
    mid = r1.astype(BF16)
    lo = (r1 - mid.astype(F32)).astype(BF16)
    return hi, mid, lo


def _bucket(dist):
    max_exact = REL_BUCKETS // 2
    d = jnp.maximum(dist, 0)
    df = jnp.maximum(d, 1).astype(F32)
    large = max_exact + (jnp.log(df / max_exact) / math.log(REL_MAX_DIST / max_exact)
                         * (REL_BUCKETS - max_exact)).astype(jnp.int32)
    large = jnp.minimum(large, REL_BUCKETS - 1)
    return jnp.where(d < max_exact, d, large)


def _top_blocks(gate, blk, n_past, axis):
    blkf = blk.astype(F32)
    sel = jnp.zeros(gate.shape, jnp.bool_)
    for _ in range(MOBA_TOPK):
        mx = jnp.max(gate, axis=axis, keepdims=True)
        idx = jnp.min(jnp.where(gate == mx, blkf, float(LANES)), axis=axis, keepdims=True)
        pick = blkf == idx
        sel = sel | (pick & (blk < n_past))
        gate = jnp.where(pick, PICKED, gate)
    return sel


def _const_spec(shape, n_grid):
    zeros = (0,) * len(shape)
    index_map = {1: lambda i: zeros, 2: lambda i, j: zeros, 3: lambda i, j, k: zeros}[n_grid]
    return pl.BlockSpec(shape, index_map, pipeline_mode=pl.Buffered(1))


def _proj_kernel(x_ref, gpre_ref, w_ref, wkt_ref, wvt_ref, wf_ref, bf_ref, tri_ref,
                 q_ref, k_ref, vt_ref, g_ref, k32_ref, v32_ref, lf_ref, cum_ref, kbar_ref,
                 carry_ref, *, tiles_per_seq, tm, d, kv_transposed):
    t = pl.program_id(0) % tiles_per_seq
    x = x_ref[...]
    h = x * lax.rsqrt(jnp.mean(x * x, axis=-1, keepdims=True) + RMS_EPS) * gpre_ref[...]
    hb = h.astype(BF16)

    q_ref[...] = _dot(hb, w_ref[:, 0:d]).astype(BF16)
    kf = _dot(hb, w_ref[:, d:2 * d])
    k_ref[...] = kf.astype(BF16)
    vt = _dot_nt(wvt_ref[...], hb)
    vt_ref[...] = vt.astype(BF16)
    if kv_transposed:
        k32_ref[0] = _dot_nt(wkt_ref[...], hb)
        v32_ref[0] = vt
    else:
        k32_ref[...] = kf
        v32_ref[...] = _dot(hb, w_ref[:, 2 * d:3 * d])
    gf = _dot(hb, w_ref[:, 3 * d:4 * d])
    g_ref[...] = (gf * jax.nn.sigmoid(gf)).astype(BF16)

    u = _dot(hb, wf_ref[...]) + bf_ref[...]
    lf = jnp.minimum(u, 0.0) - jnp.log1p(jnp.exp(-jnp.abs(u)))
    lf_ref[...] = lf

    @pl.when(t == 0)
    def _():
        carry_ref[...] = jnp.zeros_like(carry_ref)
        kbar_ref[...] = jnp.zeros_like(kbar_ref)

    hi, mid, lo = _split3(lf)
    tri = tri_ref[...]
    cum = carry_ref[0:1, :] + (_dot(tri, hi) + _dot(tri, mid) + _dot(tri, lo))
    cum_ref[...] = cum
    carry_ref[0:1, :] = cum[tm - 1:tm, :]

    for bb in range(tm // MOBA_BLOCK):
        ks = jnp.sum(kf[bb * MOBA_BLOCK:(bb + 1) * MOBA_BLOCK, d // 2:], axis=0, keepdims=True)
        kbar_ref[0, pl.ds(t * (tm // MOBA_BLOCK) + bb, 1), :] = ks * (1.0 / MOBA_BLOCK)


def _proj(x2, gpre, wq, wkt, wvt, wf, bfr, tri, *, tm, tiles_per_seq, kv_transposed):
    rows, d = x2.shape
    n_seq = rows // (tm * tiles_per_seq)
    row_blk = lambda w, dt: (pl.BlockSpec((tm, w), lambda i: (i, 0)), jax.ShapeDtypeStruct((rows, w), dt))
    kv32 = ((pl.BlockSpec((1, d, tm), lambda i: (i // tiles_per_seq, 0, i % tiles_per_seq)),
             jax.ShapeDtypeStruct((n_seq, d, tm * tiles_per_seq), F32)) if kv_transposed else row_blk(d, F32))
    outs = [row_blk(d, BF16), row_blk(d, BF16),
            (pl.BlockSpec((d, tm), lambda i: (0, i)), jax.ShapeDtypeStruct((d, rows), BF16)),
            row_blk(d, BF16), kv32, kv32, row_blk(LANES, F32), row_blk(LANES, F32),
            (pl.BlockSpec((1, LANES, d // 2), lambda i: (i // tiles_per_seq, 0, 0)),
             jax.ShapeDtypeStruct((n_seq, LANES, d // 2), F32))]
    return pl.pallas_call(
        functools.partial(_proj_kernel, tiles_per_seq=tiles_per_seq, tm=tm, d=d, kv_transposed=kv_transposed),
        grid=(rows // tm,),
        in_specs=[pl.BlockSpec((tm, d), lambda i: (i, 0)), _const_spec((1, d), 1), _const_spec((d, 4 * d), 1),
                  _const_spec((d, d), 1), _const_spec((d, d), 1), _const_spec((d, LANES), 1),
                  _const_spec((1, LANES), 1), _const_spec((tm, tm), 1)],
        out_specs=[o[0] for o in outs],
        out_shape=[o[1] for o in outs],
        scratch_shapes=[pltpu.VMEM((8, LANES), F32)],
        compiler_params=pltpu.CompilerParams(dimension_semantics=("arbitrary",),
                                             vmem_limit_bytes=VMEM_LIMIT),
        name="proj",
    )(x2, gpre, wq, wkt, wvt, wf, bfr, tri)


def _prompt_bias_kernel(rb_ref, tbl_ref):
    t = pl.program_id(0)
    h = pl.program_id(1)
    c = lax.broadcasted_iota(jnp.int32, (MOBA_BLOCK, MOBA_BLOCK), 0)
    r = lax.broadcasted_iota(jnp.int32, (MOBA_BLOCK, MOBA_BLOCK), 1)
    dist = r - c + t * MOBA_BLOCK
    bucket = _bucket(dist)
    bias = jnp.zeros((MOBA_BLOCK, MOBA_BLOCK), F32)
    for b in range(REL_BUCKETS):
        bias = jnp.where(bucket == b, rb_ref[b, h], bias)
    tbl_ref[0, 0] = jnp.where(dist >= 0, bias, NEG)


def _prompt_bias(rel_bias):
    n_heads = rel_bias.shape[1]
    return pl.pallas_call(
        _prompt_bias_kernel,
        grid=(2, n_heads),
        in_specs=[pl.BlockSpec(memory_space=pltpu.SMEM)],
        out_specs=pl.BlockSpec((1, 1, MOBA_BLOCK, MOBA_BLOCK), lambda t, h: (t, h, 0, 0)),
        out_shape=jax.ShapeDtypeStruct((2, n_heads, MOBA_BLOCK, MOBA_BLOCK), F32),
        name="prompt_bias",
    )(rel_bias)


def _decode_bias_kernel(rb_ref, decb_ref, decnew_ref, *, dq, n_heads):
    rows = 2 * dq * n_heads
    half = rows // 2

    def lookup(bucket, hrow):
        out = jnp.zeros(bucket.shape, F32)
        for hh in range(n_heads):
            for b in range(REL_BUCKETS):
                out = jnp.where((bucket == b) & (hrow == hh), rb_ref[b, hh], out)
        return out

    r = lax.broadcasted_iota(jnp.int32, (rows, MOBA_BLOCK), 0)
    c = lax.broadcasted_iota(jnp.int32, (rows, MOBA_BLOCK), 1)
    is_moba = r >= half
    qq = (r % half) // n_heads
    hrow = r % n_heads
    far = lookup(jnp.full(r.shape, REL_BUCKETS - 1, jnp.int32), hrow)
    near = lookup(_bucket(MOBA_BLOCK + qq - c), hrow)
    decb_ref[0] = jnp.where(is_moba, far, 0.0)
    decb_ref[1] = jnp.where(is_moba, near, 0.0)

    r = lax.broadcasted_iota(jnp.int32, (rows, LANES), 0)
    c = lax.broadcasted_iota(jnp.int32, (rows, LANES), 1)
    qq = (r % half) // n_heads
    dist = qq - c
    own = lookup(_bucket(dist), r % n_heads)
    decnew_ref[...] = jnp.where(dist >= 0, jnp.where(r >= half, own, 0.0), NEG)


def _decode_bias(rel_bias, dq):
    n_heads = rel_bias.shape[1]
    rows = 2 * dq * n_heads
    return pl.pallas_call(
        functools.partial(_decode_bias_kernel, dq=dq, n_heads=n_heads),
        in_specs=[pl.BlockSpec(memory_space=pltpu.SMEM)],
        out_shape=[jax.ShapeDtypeStruct((2, rows, MOBA_BLOCK), F32),
                   jax.ShapeDtypeStruct((rows, LANES), F32)],
        name="decode_bias",
    )(rel_bias)


def _stack_heads(q2):
    lane = lax.broadcasted_iota(jnp.int32, q2.shape, 1)
    qs = q2 * (1.0 / math.sqrt(HEAD_DIM))
    zero = jnp.zeros_like(qs)
    return jnp.concatenate([jnp.where(lane < HEAD_DIM, qs, zero),
                            jnp.where(lane >= HEAD_DIM, qs, zero)], axis=0).astype(BF16)


def _tile_partial(s, vt):
    m = jnp.max(s, axis=0, keepdims=True)
    p = jnp.exp(s - m)
    return m, jnp.sum(p, axis=0, keepdims=True), _dot(vt, p.astype(BF16))


def _merge_partials(carry, parts):
    m, l, acc = carry
    m_new = m
    for mt, _, _ in parts:
        m_new = jnp.maximum(m_new, mt)
    a = jnp.exp(m - m_new)
    l = a * l
    acc = a * acc
    for mt, lt, ot in parts:
        w = jnp.exp(mt - m_new)
        l = l + w * lt
        acc = acc + w * ot
    return m_new, l, acc


def _finish_heads(carry, tq):
    _, l, acc = carry
    o = acc / l
    row = lax.broadcasted_iota(jnp.int32, (LANES, tq), 0)
    return jnp.where(row < HEAD_DIM, o[:, :tq], o[:, tq:]).T


def _init_carry(tq):
    return (jnp.full((1, 2 * tq), NEG, F32), jnp.zeros((1, 2 * tq), F32), jnp.zeros((LANES, 2 * tq), F32))


def _fox_kernel(q_ref, k_ref, vt_ref, cum_ref, o_ref, ck_ref, *, tq, seq):
    hp = pl.program_id(1)
    i = pl.program_id(2)

    @pl.when(i == 0)
    def _():
        chunk = 512
        for hh in range(2):
            for c0 in range(0, seq, chunk):
                cm = cum_ref[c0:c0 + chunk, :]
                lane = lax.broadcasted_iota(jnp.int32, cm.shape, 1)
                col = jnp.sum(jnp.where(lane == 2 * hp + hh, cm, 0.0), axis=1, keepdims=True)
                ck_ref[hh, c0:c0 + chunk, :] = jnp.broadcast_to(col, cm.shape)

    qs = _stack_heads(q_ref[...])

    def tile(j, diag):
        off = pl.multiple_of(j * tq, tq)
        s = _dot_nt(k_ref[pl.ds(off, tq), :], qs)
        c0 = ck_ref[0, pl.ds(off, tq), :]
        c1 = ck_ref[1, pl.ds(off, tq), :]
        s = s - jnp.concatenate([c0] * (tq // LANES) + [c1] * (tq // LANES), axis=1)
        if diag:
            kk = lax.broadcasted_iota(jnp.int32, (tq, 2 * tq), 0)
            qq = lax.broadcasted_iota(jnp.int32, (tq, 2 * tq), 1) % tq
            s = jnp.where(kk <= qq, s, NEG)
        return _tile_partial(s, vt_ref[:, pl.ds(off, tq)])

    carry = lax.fori_loop(0, i // 2,
                          lambda j2, cr: _merge_partials(cr, [tile(2 * j2, False), tile(2 * j2 + 1, False)]),
                          _init_carry(tq))
    carry = lax.cond(i % 2 == 1,
                     lambda cr: _merge_partials(cr, [tile(i - 1, False), tile(i, True)]),
                     lambda cr: _merge_partials(cr, [tile(i, True)]),
                     carry)
    o_ref[...] = _finish_heads(carry, tq).astype(o_ref.dtype)


def _fox(q, k, vt, cum, *, batch, seq, tq):
    rows, d = q.shape
    nq = seq // tq
    n_pairs = d // 2 // LANES
    return pl.pallas_call(
        functools.partial(_fox_kernel, tq=tq, seq=seq),
        grid=(batch, n_pairs, nq),
        in_specs=[pl.BlockSpec((tq, LANES), lambda b, hp, i: (b * nq + i, hp)),
                  pl.BlockSpec((seq, LANES), lambda b, hp, i: (b, hp)),
                  pl.BlockSpec((LANES, seq), lambda b, hp, i: (hp, b)),
                  pl.BlockSpec((seq, LANES), lambda b, hp, i: (b, 0))],
        out_specs=pl.BlockSpec((tq, LANES), lambda b, hp, i: (b * nq + i, hp)),
        out_shape=jax.ShapeDtypeStruct((rows, d // 2), BF16),
        scratch_shapes=[pltpu.VMEM((2, seq, LANES), F32)],
        compiler_params=pltpu.CompilerParams(dimension_semantics=("arbitrary",) * 3,
                                             vmem_limit_bytes=VMEM_LIMIT),
        name="fox",
    )(q, k, vt, cum)


def _moba_kernel(rb_ref, q_ref, k_ref, vt_ref, kbar_ref, tbl_ref, o_ref, selb_ref, *, tq, nb_pad):
    hp = pl.program_id(1)
    i = pl.program_id(2)
    qs = _stack_heads(q_ref[...])

    hi, mid, lo = _split3(kbar_ref[0, 0:nb_pad, :])
    gate = _dot_nt(hi, qs) + _dot_nt(mid, qs) + _dot_nt(lo, qs)
    blk = lax.broadcasted_iota(jnp.int32, gate.shape, 0)
    sel = _top_blocks(jnp.where(blk < i, gate, NEG), blk, i, axis=0)
    selb_ref[...] = jnp.where(sel, 0.0, NEG)

    far = jnp.concatenate([jnp.full((1, tq), rb_ref[REL_BUCKETS - 1, 2 * hp], F32),
                           jnp.full((1, tq), rb_ref[REL_BUCKETS - 1, 2 * hp + 1], F32)], axis=1)

    def qk(j):
        off = pl.multiple_of(j * tq, tq)
        return _dot_nt(k_ref[pl.ds(off, tq), :], qs), vt_ref[:, pl.ds(off, tq)]

    def table(t):
        return jnp.concatenate([tbl_ref[t, 0], tbl_ref[t, 1]], axis=1)

    def far_tile(j):
        s, vt = qk(j)
        return _tile_partial(s + (selb_ref[pl.ds(j, 1), :] + far), vt)

    def prev_tile():
        s, vt = qk(i - 1)
        return _tile_partial(s + table(1) + selb_ref[pl.ds(i - 1, 1), :], vt)

    def diag_tile():
        s, vt = qk(i)
        return _tile_partial(s + table(0), vt)

    n_far = jnp.maximum(i - 1, 0)
    carry = lax.fori_loop(0, n_far // 2,
                          lambda j2, cr: _merge_partials(cr, [far_tile(2 * j2), far_tile(2 * j2 + 1)]),
                          _init_carry(tq))
    carry = lax.cond(
        i == 0,
        lambda cr: _merge_partials(cr, [diag_tile()]),
        lambda cr: lax.cond(n_far % 2 == 1,
                            lambda c2: _merge_partials(c2, [far_tile(n_far - 1), prev_tile(), diag_tile()]),
                            lambda c2: _merge_partials(c2, [prev_tile(), diag_tile()]),
                            cr),
        carry)
    o_ref[...] = _finish_heads(carry, tq).astype(o_ref.dtype)


def _moba(rel_bias, q, k, vt, kbar, tbl, *, batch, seq):
    rows, d = q.shape
    tq = MOBA_BLOCK
    nq = seq // tq
    nb_pad = -(-nq // 8) * 8
    n_pairs = d // 2 // LANES
    return pl.pallas_call(
        functools.partial(_moba_kernel, tq=tq, nb_pad=nb_pad),
        grid=(batch, n_pairs, nq),
        in_specs=[pl.BlockSpec(memory_space=pltpu.SMEM),
                  pl.BlockSpec((tq, LANES), lambda b, hp, i: (b * nq + i, n_pairs + hp)),
                  pl.BlockSpec((seq, LANES), lambda b, hp, i: (b, n_pairs + hp)),
                  pl.BlockSpec((LANES, seq), lambda b, hp, i: (n_pairs + hp, b)),
                  pl.BlockSpec((1, LANES, LANES), lambda b, hp, i: (b, 0, hp)),
                  pl.BlockSpec((2, 2, tq, tq), lambda b, hp, i: (0, hp, 0, 0))],
        out_specs=pl.BlockSpec((tq, LANES), lambda b, hp, i: (b * nq + i, hp)),
        out_shape=jax.ShapeDtypeStruct((rows, d // 2), BF16),
        scratch_shapes=[pltpu.VMEM((nb_pad, 2 * tq), F32)],
        compiler_params=pltpu.CompilerParams(dimension_semantics=("arbitrary",) * 3,
                                             vmem_limit_bytes=VMEM_LIMIT),
        name="moba",
    )(rel_bias, q, k, vt, kbar, tbl)


def _merge_kernel(x_ref, oa_ref, ob_ref, g_ref, w_ref, gpost_ref, out_ref, *, d):
    g = g_ref[...].astype(F32)
    ya = (oa_ref[...].astype(F32) * g[:, :d // 2]).astype(BF16)
    yb = (ob_ref[...].astype(F32) * g[:, d // 2:]).astype(BF16)
    y = _dot(ya, w_ref[0:d // 2, :]) + _dot(yb, w_ref[d // 2:, :])
    n = y * lax.rsqrt(jnp.mean(y * y, axis=-1, keepdims=True) + RMS_EPS) * gpost_ref[...]
    out_ref[...] = x_ref[...] + n


def _merge(x2, oa, ob, ob_col, g, wo, gpost, *, tm):
    rows, d = x2.shape
    return pl.pallas_call(
        functools.partial(_merge_kernel, d=d),
        grid=(rows // tm,),
        in_specs=[pl.BlockSpec((tm, d), lambda i: (i, 0)),
                  pl.BlockSpec((tm, d // 2), lambda i: (i, 0)),
                  pl.BlockSpec((tm, d // 2), lambda i: (i, ob_col)),
                  pl.BlockSpec((tm, d), lambda i: (i, 0)),
                  _const_spec((d, d), 1), _const_spec((1, d), 1)],
        out_specs=pl.BlockSpec((tm, d), lambda i: (i, 0)),
        out_shape=jax.ShapeDtypeStruct((rows, d), F32),
        compiler_params=pltpu.CompilerParams(dimension_semantics=("arbitrary",),
                                             vmem_limit_bytes=VMEM_LIMIT),
        name="merge",
    )(x2, oa, ob, g, wo, gpost)


def _decode_kernel(pt_ref, q_ref, kn_ref, vn_ref, lfn_ref, *rest, nblk, dq, d, bps, ppb):
    del pt_ref
    n_pg = bps * ppb
    k_refs, v_refs, l_refs = rest[0:n_pg], rest[n_pg:2 * n_pg], rest[2 * n_pg:3 * n_pg]
    (decb_ref, decnew_ref, tri_ref, o_ref,
     qbd_ref, m_ref, l_ref, acc_ref, carry_ref, pm_ref, pls_ref, po_ref, gate_ref) = rest[3 * n_pg:]
    j = pl.program_id(1)
    half = d // 2
    n_heads = half // HEAD_DIM
    hr = dq * n_heads

    def bcast(x, ref):
        return jnp.broadcast_to(x, ref.shape)

    @pl.when(j == 0)
    def _init():
        q4 = q_ref[0].astype(F32) * (1.0 / math.sqrt(HEAD_DIM))
        rows = [jnp.broadcast_to(q4[qq:qq + 1], (n_heads, d)) for qq in range(dq)]
        qrep = jnp.concatenate(rows + rows, axis=0)
        r = lax.broadcasted_iota(jnp.int32, (2 * hr, d), 0)
        c = lax.broadcasted_iota(jnp.int32, (2 * hr, d), 1)
        head = (r // hr) * n_heads + r % n_heads
        qbd_ref[...] = jnp.where(c // HEAD_DIM == head, qrep, 0.0).astype(BF16)
        m_ref[...] = jnp.full(m_ref.shape, NEG, F32)
        l_ref[...] = jnp.zeros_like(l_ref)
        acc_ref[...] = jnp.zeros_like(acc_ref)
        carry_ref[...] = jnp.zeros_like(carry_ref)
        pm_ref[...] = jnp.full(pm_ref.shape, NEG, F32)
        pls_ref[...] = jnp.zeros_like(pls_ref)
        gate_ref[...] = jnp.zeros_like(gate_ref)

    qf = qbd_ref[0:hr, 0:half]
    qm = qbd_ref[hr:, half:]
    lane = lax.broadcasted_iota(jnp.int32, (hr, LANES), 1)

    def fox_bias(cum):
        return jnp.concatenate([cum[0:n_heads]] * dq, axis=0)

    def partial_softmax(s, vt16):
        mb = jnp.max(s, axis=1, keepdims=True)
        p = jnp.exp(s - mb)
        return mb, jnp.sum(p, axis=1, keepdims=True), _dot_nt(p.astype(BF16), vt16)

    def fox_merge(parts):
        m_old = m_ref[:, 0:1]
        m_new = m_old
        for mb, _, _ in parts:
            m_new = jnp.maximum(m_new, mb)
        a = jnp.exp(m_old - m_new)
        l_new = a * l_ref[:, 0:1]
        acc = a * acc_ref[...]
        for mb, lb, ob in parts:
            b = jnp.exp(mb - m_new)
            l_new = l_new + b * lb
            acc = acc + b * ob
        l_ref[...] = bcast(l_new, l_ref)
        acc_ref[...] = acc
        m_ref[...] = bcast(m_new, m_ref)

    run = carry_ref[:, 0:1]
    gate_new, pm_new, pls_new = gate_ref[...], pm_ref[...], pls_ref[...]
    fox_parts = []
    for u in range(bps):
        jb = j * bps + u
        kt16 = jnp.concatenate([r[0] for r in k_refs[u * ppb:(u + 1) * ppb]], axis=1).astype(BF16)
        vt16 = jnp.concatenate([r[0] for r in v_refs[u * ppb:(u + 1) * ppb]], axis=1).astype(BF16)

        lf = jnp.concatenate([r[0] for r in l_refs[u * ppb:(u + 1) * ppb]], axis=1)
        hi, mid, lo = _split3(jnp.concatenate([lf, jnp.zeros_like(lf)], axis=0))
        tri = tri_ref[...]
        cum = run + (_dot(hi, tri) + _dot(mid, tri) + _dot(lo, tri))
        run = cum[:, MOBA_BLOCK - 1:MOBA_BLOCK]

        s_f = _dot(qf, kt16[0:half]) - fox_bias(cum)
        s_raw = _dot(qm, kt16[half:])
        gate_new = jnp.where(lane == jb, jnp.sum(s_raw, axis=1, keepdims=True), gate_new)
        s_m = s_raw + decb_ref[jnp.where(jb == nblk - 1, 1, 0)][hr:]
        fox_parts.append(partial_softmax(s_f, vt16[0:half]))
        mb, lb, ob = partial_softmax(s_m, vt16[half:])
        pm_new = jnp.where(lane == jb, mb, pm_new)
        pls_new = jnp.where(lane == jb, lb, pls_new)
        po_ref[jb] = ob
    carry_ref[...] = bcast(run, carry_ref)
    gate_ref[...] = gate_new
    pm_ref[...] = pm_new
    pls_ref[...] = pls_new
    fox_merge(fox_parts)

    @pl.when(j == nblk // bps - 1)
    def _finish():
        pad = jnp.zeros((LANES - kn_ref.shape[1], d), F32)
        kn = jnp.concatenate([kn_ref[0], pad], axis=0).astype(BF16)
        vn = jnp.concatenate([vn_ref[0], pad], axis=0).astype(BF16)
        dn = decnew_ref[...]

        def new_softmax(s, v16):
            mb = jnp.max(s, axis=1, keepdims=True)
            p = jnp.exp(s - mb)
            return mb, jnp.sum(p, axis=1, keepdims=True), _dot(p.astype(BF16), v16)

        def heads_to_cols(o):
            r = lax.broadcasted_iota(jnp.int32, o.shape, 0)
            c = lax.broadcasted_iota(jnp.int32, o.shape, 1)
            o = jnp.where(c // HEAD_DIM == r % n_heads, o, 0.0)
            return jnp.concatenate([jnp.sum(o[qq * n_heads:(qq + 1) * n_heads], axis=0, keepdims=True)
                                    for qq in range(dq)], axis=0)

        lfn = lfn_ref[0]
        hi, mid, lo = _split3(jnp.concatenate([lfn, jnp.zeros_like(lfn)], axis=0))
        tri = tri_ref[0:LANES, 0:LANES]
        cn = carry_ref[:, 0:1] + (_dot(hi, tri) + _dot(mid, tri) + _dot(lo, tri))
        s_n = _dot_nt(qf, kn[:, 0:half]) - fox_bias(cn) + dn[0:hr]
        fox_merge([new_softmax(s_n, vn[:, 0:half])])
        o_fox = heads_to_cols(acc_ref[...] / l_ref[:, 0:1])

        gate = jnp.where(lane < nblk, gate_ref[...], jnp.where(lane == nblk, NEG, PICKED))
        sel = _top_blocks(gate, lane, nblk, axis=1)
        m_n, l_n, o_n = new_softmax(_dot_nt(qm, kn[:, half:]) + dn[hr:], vn[:, half:])
        pm_sel = jnp.where(sel, pm_ref[...], NEG)
        m_tot = jnp.maximum(jnp.max(pm_sel, axis=1, keepdims=True), m_n)
        w = jnp.exp(pm_sel - m_tot)
        w_n = jnp.exp(m_n - m_tot)
        l_tot = jnp.sum(w * pls_ref[...], axis=1, keepdims=True) + w_n * l_n
        o_tot = w_n * o_n
        for jj in range(nblk):
            o_tot = o_tot + w[:, jj:jj + 1] * po_ref[jj]
        o_moba = heads_to_cols(o_tot / l_tot)

        o_ref[0] = jnp.concatenate([o_fox, o_moba], axis=1)


def _decode(page_table, q3, kn3, vn3, lfn3, ckt, cvt, clt, decb, decnew, tri, *, page):
    n, dq, d = q3.shape
    n_pages = page_table.shape[1]
    ppb = MOBA_BLOCK // page
    nblk = n_pages // ppb
    bps = DECODE_BLOCKS_PER_STEP if nblk % DECODE_BLOCKS_PER_STEP == 0 else 1
    n_pg = bps * ppb
    half = d // 2
    hr = dq * (half // HEAD_DIM)
    kv_spec = lambda o: pl.BlockSpec((1, d, page), lambda s, j, pt: (pt[s, n_pg * j + o], 0, 0))
    lf_spec = lambda o: pl.BlockSpec((1, 8, page), lambda s, j, pt: (pt[s, n_pg * j + o], 0, 0))
    seq_spec = lambda a: pl.BlockSpec((1,) + a.shape[1:], lambda s, j, pt: (s, 0, 0))
    const = lambda a: pl.BlockSpec(a.shape, lambda s, j, pt: (0,) * a.ndim)
    grid_spec = pltpu.PrefetchScalarGridSpec(
        num_scalar_prefetch=1,
        grid=(n, nblk // bps),
        in_specs=([seq_spec(q3), seq_spec(kn3), seq_spec(vn3), seq_spec(lfn3)]
                  + [kv_spec(o) for o in range(n_pg)] * 2 + [lf_spec(o) for o in range(n_pg)]
                  + [const(decb), const(decnew), const(tri)]),
        out_specs=pl.BlockSpec((1, dq, d), lambda s, j, pt: (s, 0, 0)),
        scratch_shapes=[pltpu.VMEM((2 * hr, d), BF16),
                        pltpu.VMEM((hr, LANES), F32),
                        pltpu.VMEM((hr, LANES), F32),
                        pltpu.VMEM((hr, half), F32),
                        pltpu.VMEM((16, LANES), F32),
                        pltpu.VMEM((hr, LANES), F32),
                        pltpu.VMEM((hr, LANES), F32),
                        pltpu.VMEM((nblk, hr, half), F32),
                        pltpu.VMEM((hr, LANES), F32)])
    return pl.pallas_call(
        functools.partial(_decode_kernel, nblk=nblk, dq=dq, d=d, bps=bps, ppb=ppb),
        grid_spec=grid_spec,
        out_shape=jax.ShapeDtypeStruct((n, dq, d), F32),
        compiler_params=pltpu.CompilerParams(dimension_semantics=("arbitrary", "arbitrary"),
                                             vmem_limit_bytes=VMEM_LIMIT),
        name="decode",
    )(page_table, q3, kn3, vn3, lfn3, *([ckt] * n_pg), *([cvt] * n_pg), *([clt] * n_pg), decb, decnew, tri)


def _tri(n, lower):
    r = lax.broadcasted_iota(jnp.int32, (n, n), 0)
    c = lax.broadcasted_iota(jnp.int32, (n, n), 1)
    return ((r >= c) if lower else (r <= c)).astype(BF16)


def kernel(x_prompt, x_sample, cache_k, cache_v, cache_logf, page_table, g_pre, w_in, b_f, rel_bias, w_out, g_post):
    batch, seq, d = x_prompt.shape
    n_dec, dq, _ = x_sample.shape
    depth, n_pool, page = cache_k.shape[0], cache_k.shape[1], cache_k.shape[2]
    n_heads = d // HEAD_DIM
    fox_heads = n_heads // 2
    assert d == 2 * fox_heads * HEAD_DIM and fox_heads == 8 and rel_bias.shape == (REL_BUCKETS, fox_heads)
    assert seq % 512 == 0 and MOBA_BLOCK % page == 0 and (page_table.shape[1] * page) % MOBA_BLOCK == 0
    assert dq <= 8 and (n_dec * dq) % 8 == 0 and page_table.shape[1] * page // MOBA_BLOCK < LANES

    tm = 512
    tm_s = n_dec * dq if n_dec * dq <= 512 else 512
    assert (n_dec * dq) % tm_s == 0
    tri_p, tri_s, tri_d = _tri(tm, True), _tri(tm_s, True), _tri(MOBA_BLOCK, False)
    tbl = _prompt_bias(rel_bias)
    decb, decnew = _decode_bias(rel_bias, dq)

    xp = x_prompt.reshape(batch * seq, d)
    xs = x_sample.reshape(n_dec * dq, d)
    kp_l, vp_l, lp_l, ks_l, vs_l, ls_l = [], [], [], [], [], []
    for l in range(depth):
        wq = w_in[l][:, :4 * d].astype(BF16)
        wkt = w_in[l][:, d:2 * d].T.astype(BF16)
        wvt = w_in[l][:, 2 * d:3 * d].T.astype(BF16)
        wf = jnp.pad(w_in[l][:, 4 * d:], ((0, 0), (0, LANES - fox_heads))).astype(BF16)
        bfr = jnp.pad(b_f[l], (0, LANES - fox_heads))[None, :].astype(F32)
        gpre = g_pre[l][None, :]
        gpost = g_post[l][None, :]
        wo = w_out[l].astype(BF16)

        q, k, vt, g, kt32, vt32, lf, cum, kbar = _proj(xp, gpre, wq, wkt, wvt, wf, bfr, tri_p, tm=tm,
                                                       tiles_per_seq=seq // tm, kv_transposed=True)
        o_fox = _fox(q, k, vt, cum, batch=batch, seq=seq, tq=MOBA_BLOCK)
        o_moba = _moba(rel_bias, q, k, vt, kbar, tbl, batch=batch, seq=seq)
        kp_l.append(kt32.reshape(batch, n_heads, HEAD_DIM, seq).transpose(0, 3, 1, 2))
        vp_l.append(vt32.reshape(batch, n_heads, HEAD_DIM, seq).transpose(0, 3, 1, 2))
        lp_l.append(lf[:, :fox_heads].reshape(batch, seq, fox_heads))

        qs, _, _, gs, k32s, v32s, lfs, _, _ = _proj(xs, gpre, wq, wkt, wvt, wf, bfr, tri_s, tm=tm_s,
                                                    tiles_per_seq=1, kv_transposed=False)
        lfs = lfs[:, :fox_heads]
        pad_rows = lambda a: jnp.pad(a.reshape(n_dec, dq, d), ((0, 0), (0, 8 - dq), (0, 0)))
        lfn3 = jnp.pad(lfs.reshape(n_dec, dq, fox_heads).transpose(0, 2, 1), ((0, 0), (0, 0), (0, LANES - dq)))
        o_s = _decode(page_table, qs.reshape(n_dec, dq, d), pad_rows(k32s), pad_rows(v32s), lfn3,
                      cache_k[l].transpose(0, 2, 3, 1).reshape(n_pool, d, page),
                      cache_v[l].transpose(0, 2, 3, 1).reshape(n_pool, d, page),
                      cache_logf[l].transpose(0, 2, 1), decb, decnew, tri_d, page=page)
        o_s = o_s.reshape(n_dec * dq, d)
        ks_l.append(k32s.reshape(n_dec, dq, n_heads, HEAD_DIM))
        vs_l.append(v32s.reshape(n_dec, dq, n_heads, HEAD_DIM))
        ls_l.append(lfs.reshape(n_dec, dq, fox_heads))

        xp = _merge(xp, o_fox, o_moba, 0, g, wo, gpost, tm=tm)
        xs = _merge(xs, o_s, o_s, 1, gs, wo, gpost, tm=tm_s)

    return (xp.reshape(batch, seq, d), xs.reshape(n_dec, dq, d), jnp.stack(kp_l), jnp.stack(vp_l),
            jnp.stack(lp_l), jnp.stack(ks_l), jnp.stack(vs_l), jnp.stack(ls_l))
```

```python
import functools
import math

import jax
import jax.numpy as jnp
from jax import lax
from jax.experimental import pallas as pl
from jax.experimental.pallas import tpu as pltpu

HEAD_DIM = 64
LANES = 128
MOBA_BLOCK = 256
MOBA_TOPK = 3
REL_BUCKETS = 32
REL_MAX_DIST = 128
RMS_EPS = 1e-6
NEG = -1e30
PICKED = -3e38
VMEM_LIMIT = 56 * 1024 * 1024
DECODE_BLOCKS_PER_STEP = 8

F32 = jnp.float32
BF16 = jnp.bfloat16
NT = (((1,), (1,)), ((), ()))


def _dot(a, b):
    return jnp.dot(a, b, preferred_element_type=F32)


def _dot_nt(a, b):
    return lax.dot_general(a, b, NT, preferred_element_type=F32)


def _split3(x):
    hi = x.astype(BF16)
    r1 = x - hi.astype(F32)
    mid = r1.astype(BF16)
    lo = (r1 - mid.astype(F32)).astype(BF16)
    return hi, mid, lo


def _bucket(dist):
    max_exact = REL_BUCKETS // 2
    d = jnp.maximum(dist, 0)
    df = jnp.maximum(d, 1).astype(F32)
    large = max_exact + (jnp.log(df / max_exact) / math.log(REL_MAX_DIST / max_exact)
                         * (REL_BUCKETS - max_exact)).astype(jnp.int32)
    large = jnp.minimum(large, REL_BUCKETS - 1)
    return jnp.where(d < max_exact, d, large)


def _top_blocks(gate, blk, n_past, axis):
    blkf = blk.astype(F32)
    sel = jnp.zeros(gate.shape, jnp.bool_)
    for _ in range(MOBA_TOPK):
        mx = jnp.max(gate, axis=axis, keepdims=True)
        idx = jnp.min(jnp.where(gate == mx, blkf, float(LANES)), axis=axis, keepdims=True)
        pick = blkf == idx
        sel = sel | (pick & (blk < n_past))
        gate = jnp.where(pick, PICKED, gate)
    return sel


def _const_spec(shape, n_grid):
    zeros = (0,) * len(shape)
    index_map = {1: lambda i: zeros, 2: lambda i, j: zeros, 3: lambda i, j, k: zeros}[n_grid]
    return pl.BlockSpec(shape, index_map, pipeline_mode=pl.Buffered(1))


def _proj_kernel(x_ref, gpre_ref, w_ref, wkt_ref, wvt_ref, wf_ref, bf_ref, tri_ref,
                 q_ref, k_ref, vt_ref, g_ref, k32_ref, v32_ref, lf_ref, cum_ref, kbar_ref,
                 carry_ref, *, tiles_per_seq, tm, d, kv_transposed):
    t = pl.program_id(0) % tiles_per_seq
    x = x_ref[...]
    h = x * lax.rsqrt(jnp.mean(x * x, axis=-1, keepdims=True) + RMS_EPS) * gpre_ref[...]
    hb = h.astype(BF16)

    q_ref[...] = _dot(hb, w_ref[:, 0:d]).astype(BF16)
    kf = _dot(hb, w_ref[:, d:2 * d])
    k_ref[...] = kf.astype(BF16)
    vt = _dot_nt(wvt_ref[...], hb)
    vt_ref[...] = vt.astype(BF16)
    if kv_transposed:
        k32_ref[0] = _dot_nt(wkt_ref[...], hb)
        v32_ref[0] = vt
    else:
        k32_ref[...] = kf
        v32_ref[...] = _dot(hb, w_ref[:, 2 * d:3 * d])
    gf = _dot(hb, w_ref[:, 3 * d:4 * d])
    g_ref[...] = (gf * jax.nn.sigmoid(gf)).astype(BF16)

    u = _dot(hb, wf_ref[...]) + bf_ref[...]
    lf = jnp.minimum(u, 0.0) - jnp.log1p(jnp.exp(-jnp.abs(u)))
    lf_ref[...] = lf

    @pl.when(t == 0)
    def _():
        carry_ref[...] = jnp.zeros_like(carry_ref)
        kbar_ref[...] = jnp.zeros_like(kbar_ref)

    hi, mid, lo = _split3(lf)
    tri = tri_ref[...]
    cum = carry_ref[0:1, :] + (_dot(tri, hi) + _dot(tri, mid) + _dot(tri, lo))
    cum_ref[...] = cum
    carry_ref[0:1, :] = cum[tm - 1:tm, :]

    for bb in range(tm // MOBA_BLOCK):
        ks = jnp.sum(kf[bb * MOBA_BLOCK:(bb + 1) * MOBA_BLOCK, d // 2:], axis=0, keepdims=True)
        kbar_ref[0, pl.ds(t * (tm // MOBA_BLOCK) + bb, 1), :] = ks * (1.0 / MOBA_BLOCK)


def _proj(x2, gpre, wq, wkt, wvt, wf, bfr, tri, *, tm, tiles_per_seq, kv_transposed):
    rows, d = x2.shape
    n_seq = rows // (tm * tiles_per_seq)
    row_blk = lambda w, dt: (pl.BlockSpec((tm, w), lambda i: (i, 0)), jax.ShapeDtypeStruct((rows, w), dt))
    kv32 = ((pl.BlockSpec((1, d, tm), lambda i: (i // tiles_per_seq, 0, i % tiles_per_seq)),
             jax.ShapeDtypeStruct((n_seq, d, tm * tiles_per_seq), F32)) if kv_transposed else row_blk(d, F32))
    outs = [row_blk(d, BF16), row_blk(d, BF16),
            (pl.BlockSpec((d, tm), lambda i: (0, i)), jax.ShapeDtypeStruct((d, rows), BF16)),
            row_blk(d, BF16), kv32, kv32, row_blk(LANES, F32), row_blk(LANES, F32),
            (pl.BlockSpec((1, LANES, d // 2), lambda i: (i // tiles_per_seq, 0, 0)),
             jax.ShapeDtypeStruct((n_seq, LANES, d // 2), F32))]
    return pl.pallas_call(
        functools.partial(_proj_kernel, tiles_per_seq=tiles_per_seq, tm=tm, d=d, kv_transposed=kv_transposed),
        grid=(rows // tm,),
        in_specs=[pl.BlockSpec((tm, d), lambda i: (i, 0)), _const_spec((1, d), 1), _const_spec((d, 4 * d), 1),
                  _const_spec((d, d), 1), _const_spec((d, d), 1), _const_spec((d, LANES), 1),
                  _const_spec((1, LANES), 1), _const_spec((tm, tm), 1)],
        out_specs=[o[0] for o in outs],
        out_shape=[o[1] for o in outs],
        scratch_shapes=[pltpu.VMEM((8, LANES), F32)],
        compiler_params=pltpu.CompilerParams(dimension_semantics=("arbitrary",),
                                             vmem_limit_bytes=VMEM_LIMIT),
        name="proj",
    )(x2, gpre, wq, wkt, wvt, wf, bfr, tri)


def _prompt_bias_kernel(rb_ref, tbl_ref):
    t = pl.program_id(0)
    h = pl.program_id(1)
    c = lax.broadcasted_iota(jnp.int32, (MOBA_BLOCK, MOBA_BLOCK), 0)
    r = lax.broadcasted_iota(jnp.int32, (MOBA_BLOCK, MOBA_BLOCK), 1)
    dist = r - c + t * MOBA_BLOCK
    bucket = _bucket(dist)
    bias = jnp.zeros((MOBA_BLOCK, MOBA_BLOCK), F32)
    for b in range(REL_BUCKETS):
        bias = jnp.where(bucket == b, rb_ref[b, h], bias)
    tbl_ref[0, 0] = jnp.where(dist >= 0, bias, NEG)


def _prompt_bias(rel_bias):
    n_heads = rel_bias.shape[1]
    return pl.pallas_call(
        _prompt_bias_kernel,
        grid=(2, n_heads),
        in_specs=[pl.BlockSpec(memory_space=pltpu.SMEM)],
        out_specs=pl.BlockSpec((1, 1, MOBA_BLOCK, MOBA_BLOCK), lambda t, h: (t, h, 0, 0)),
        out_shape=jax.ShapeDtypeStruct((2, n_heads, MOBA_BLOCK, MOBA_BLOCK), F32),
        name="prompt_bias",
    )(rel_bias)


def _decode_bias_kernel(rb_ref, decb_ref, decnew_ref, *, dq, n_heads):
    rows = 2 * dq * n_heads
    half = rows // 2

    def lookup(bucket, hrow):
        out = jnp.zeros(bucket.shape, F32)
        for hh in range(n_heads):
            for b in range(REL_BUCKETS):
                out = jnp.where((bucket == b) & (hrow == hh), rb_ref[b, hh], out)
        return out

    r = lax.broadcasted_iota(jnp.int32, (rows, MOBA_BLOCK), 0)
    c = lax.broadcasted_iota(jnp.int32, (rows, MOBA_BLOCK), 1)
    is_moba = r >= half
    qq = (r % half) // n_heads
    hrow = r % n_heads
    far = lookup(jnp.full(r.shape, REL_BUCKETS - 1, jnp.int32), hrow)
    near = lookup(_bucket(MOBA_BLOCK + qq - c), hrow)
    decb_ref[0] = jnp.where(is_moba, far, 0.0)
    decb_ref[1] = jnp.where(is_moba, near, 0.0)

    r = lax.broadcasted_iota(jnp.int32, (rows, LANES), 0)
    c = lax.broadcasted_iota(jnp.int32, (rows, LANES), 1)
    qq = (r % half) // n_heads
    dist = qq - c
    own = lookup(_bucket(dist), r % n_heads)
    decnew_ref[...] = jnp.where(dist >= 0, jnp.where(r >= half, own, 0.0), NEG)


def _decode_bias(rel_bias, dq):
    n_heads = rel_bias.shape[1]
    rows = 2 * dq * n_heads
    return pl.pallas_call(
        functools.partial(_decode_bias_kernel, dq=dq, n_heads=n_heads),
        in_specs=[pl.BlockSpec(memory_space=pltpu.SMEM)],
        out_shape=[jax.ShapeDtypeStruct((2, rows, MOBA_BLOCK), F32),
                   jax.ShapeDtypeStruct((rows, LANES), F32)],
        name="decode_bias",
    )(rel_bias)


def _stack_heads(q2):
    lane = lax.broadcasted_iota(jnp.int32, q2.shape, 1)
    qs = q2 * (1.0 / math.sqrt(HEAD_DIM))
    zero = jnp.zeros_like(qs)
    return jnp.concatenate([jnp.where(lane < HEAD_DIM, qs, zero),
                            jnp.where(lane >= HEAD_DIM, qs, zero)], axis=0).astype(BF16)


def _tile_partial(s, vt):
    m = jnp.max(s, axis=0, keepdims=True)
    p = jnp.exp(s - m)
    return m, jnp.sum(p, axis=0, keepdims=True), _dot(vt, p.astype(BF16))


def _merge_partials(carry, parts):
    m, l, acc = carry
    m_new = m
    for mt, _, _ in parts:
        m_new = jnp.maximum(m_new, mt)
    a = jnp.exp(m - m_new)
    l = a * l
    acc = a * acc
    for mt, lt, ot in parts:
        w = jnp.exp(mt - m_new)
        l = l + w * lt
        acc = acc + w * ot
    return m_new, l, acc


def _finish_heads(carry, tq):
    _, l, acc = carry
    o = acc / l
    row = lax.broadcasted_iota(jnp.int32, (LANES, tq), 0)
    return jnp.where(row < HEAD_DIM, o[:, :tq], o[:, tq:]).T


def _init_carry(tq):
    return (jnp.full((1, 2 * tq), NEG, F32), jnp.zeros((1, 2 * tq), F32), jnp.zeros((LANES, 2 * tq), F32))


def _fox_kernel(q_ref, k_ref, vt_ref, cum_ref, o_ref, ck_ref, *, tq, seq):
    hp = pl.program_id(1)
    i = pl.program_id(2)

    @pl.when(i == 0)
    def _():
        chunk = 512
        for hh in range(2):
            for c0 in range(0, seq, chunk):
                cm = cum_ref[c0:c0 + chunk, :]
                lane = lax.broadcasted_iota(jnp.int32, cm.shape, 1)
                col = jnp.sum(jnp.where(lane == 2 * hp + hh, cm, 0.0), axis=1, keepdims=True)
                ck_ref[hh, c0:c0 + chunk, :] = jnp.broadcast_to(col, cm.shape)

    qs = _stack_heads(q_ref[...])

    def tile(j, diag):
        off = pl.multiple_of(j * tq, tq)
        s = _dot_nt(k_ref[pl.ds(off, tq), :], qs)
        c0 = ck_ref[0, pl.ds(off, tq), :]
        c1 = ck_ref[1, pl.ds(off, tq), :]
        s = s - jnp.concatenate([c0] * (tq // LANES) + [c1] * (tq // LANES), axis=1)
        if diag:
            kk = lax.broadcasted_iota(jnp.int32, (tq, 2 * tq), 0)
            qq = lax.broadcasted_iota(jnp.int32, (tq, 2 * tq), 1) % tq
            s = jnp.where(kk <= qq, s, NEG)
        return _tile_partial(s, vt_ref[:, pl.ds(off, tq)])

    carry = lax.fori_loop(0, i // 2,
                          lambda j2, cr: _merge_partials(cr, [tile(2 * j2, False), tile(2 * j2 + 1, False)]),
                          _init_carry(tq))
    carry = lax.cond(i % 2 == 1,
                     lambda cr: _merge_partials(cr, [tile(i - 1, False), tile(i, True)]),
                     lambda cr: _merge_partials(cr, [tile(i, True)]),
                     carry)
    o_ref[...] = _finish_heads(carry, tq).astype(o_ref.dtype)


def _fox(q, k, vt, cum, *, batch, seq, tq):
    rows, d = q.shape
    nq = seq // tq
    n_pairs = d // 2 // LANES
    return pl.pallas_call(
        functools.partial(_fox_kernel, tq=tq, seq=seq),
        grid=(batch, n_pairs, nq),
        in_specs=[pl.BlockSpec((tq, LANES), lambda b, hp, i: (b * nq + i, hp)),
                  pl.BlockSpec((seq, LANES), lambda b, hp, i: (b, hp)),
                  pl.BlockSpec((LANES, seq), lambda b, hp, i: (hp, b)),
                  pl.BlockSpec((seq, LANES), lambda b, hp, i: (b, 0))],
        out_specs=pl.BlockSpec((tq, LANES), lambda b, hp, i: (b * nq + i, hp)),
        out_shape=jax.ShapeDtypeStruct((rows, d // 2), BF16),
        scratch_shapes=[pltpu.VMEM((2, seq, LANES), F32)],
        compiler_params=pltpu.CompilerParams(dimension_semantics=("arbitrary",) * 3,
                                             vmem_limit_bytes=VMEM_LIMIT),
        name="fox",
    )(q, k, vt, cum)


def _moba_kernel(rb_ref, q_ref, k_ref, vt_ref, kbar_ref, tbl_ref, o_ref, selb_ref, *, tq, nb_pad):
    hp = pl.program_id(1)
    i = pl.program_id(2)
    qs = _stack_heads(q_ref[...])

    hi, mid, lo = _split3(kbar_ref[0, 0:nb_pad, :])
    gate = _dot_nt(hi, qs) + _dot_nt(mid, qs) + _dot_nt(lo, qs)
    blk = lax.broadcasted_iota(jnp.int32, gate.shape, 0)
    sel = _top_blocks(jnp.where(blk < i, gate, NEG), blk, i, axis=0)
    selb_ref[...] = jnp.where(sel, 0.0, NEG)

    far = jnp.concatenate([jnp.full((1, tq), rb_ref[REL_BUCKETS - 1, 2 * hp], F32),
                           jnp.full((1, tq), rb_ref[REL_BUCKETS - 1, 2 * hp + 1], F32)], axis=1)

    def qk(j):
        off = pl.multiple_of(j * tq, tq)
        return _dot_nt(k_ref[pl.ds(off, tq), :], qs), vt_ref[:, pl.ds(off, tq)]

    def table(t):
        return jnp.concatenate([tbl_ref[t, 0], tbl_ref[t, 1]], axis=1)

    def far_tile(j):
        s, vt = qk(j)
        return _tile_partial(s + (selb_ref[pl.ds(j, 1), :] + far), vt)

    def prev_tile():
        s, vt = qk(i - 1)
        return _tile_partial(s + table(1) + selb_ref[pl.ds(i - 1, 1), :], vt)

    def diag_tile():
        s, vt = qk(i)
        return _tile_partial(s + table(0), vt)

    n_far = jnp.maximum(i - 1, 0)
    carry = lax.fori_loop(0, n_far // 2,
                          lambda j2, cr: _merge_partials(cr, [far_tile(2 * j2), far_tile(2 * j2 + 1)]),
                          _init_carry(tq))
    carry = lax.cond(
        i == 0,
        lambda cr: _merge_partials(cr, [diag_tile()]),
        lambda cr: lax.cond(n_far % 2 == 1,
                            lambda c2: _merge_partials(c2, [far_tile(n_far - 1), prev_tile(), diag_tile()]),
                            lambda c2: _merge_partials(c2, [prev_tile(), diag_tile()]),
                            cr),
        carry)
    o_ref[...] = _finish_heads(carry, tq).astype(o_ref.dtype)


def _moba(rel_bias, q, k, vt, kbar, tbl, *, batch, seq):
    rows, d = q.shape
    tq = MOBA_BLOCK
    nq = seq // tq
    nb_pad = -(-nq // 8) * 8
    n_pairs = d // 2 // LANES
    return pl.pallas_call(
        functools.partial(_moba_kernel, tq=tq, nb_pad=nb_pad),
        grid=(batch, n_pairs, nq),
        in_specs=[pl.BlockSpec(memory_space=pltpu.SMEM),
                  pl.BlockSpec((tq, LANES), lambda b, hp, i: (b * nq + i, n_pairs + hp)),
                  pl.BlockSpec((seq, LANES), lambda b, hp, i: (b, n_pairs + hp)),
                  pl.BlockSpec((LANES, seq), lambda b, hp, i: (n_pairs + hp, b)),
                  pl.BlockSpec((1, LANES, LANES), lambda b, hp, i: (b, 0, hp)),
                  pl.BlockSpec((2, 2, tq, tq), lambda b, hp, i: (0, hp, 0, 0))],
        out_specs=pl.BlockSpec((tq, LANES), lambda b, hp, i: (b * nq + i, hp)),
        out_shape=jax.ShapeDtypeStruct((rows, d // 2), BF16),
        scratch_shapes=[pltpu.VMEM((nb_pad, 2 * tq), F32)],
        compiler_params=pltpu.CompilerParams(dimension_semantics=("arbitrary",) * 3,
                                             vmem_limit_bytes=VMEM_LIMIT),
        name="moba",
    )(rel_bias, q, k, vt, kbar, tbl)


def _merge_kernel(x_ref, oa_ref, ob_ref, g_ref, w_ref, gpost_ref, out_ref, *, d):
    g = g_ref[...].astype(F32)
    ya = (oa_ref[...].astype(F32) * g[:, :d // 2]).astype(BF16)
    yb = (ob_ref[...].astype(F32) * g[:, d // 2:]).astype(BF16)
    y = _dot(ya, w_ref[0:d // 2, :]) + _dot(yb, w_ref[d // 2:, :])
    n = y * lax.rsqrt(jnp.mean(y * y, axis=-1, keepdims=True) + RMS_EPS) * gpost_ref[...]
    out_ref[...] = x_ref[...] + n


def _merge(x2, oa, ob, ob_col, g, wo, gpost, *, tm):
    rows, d = x2.shape
    return pl.pallas_call(
        functools.partial(_merge_kernel, d=d),
        grid=(rows // tm,),
        in_specs=[pl.BlockSpec((tm, d), lambda i: (i, 0)),
                  pl.BlockSpec((tm, d // 2), lambda i: (i, 0)),
                  pl.BlockSpec((tm, d // 2), lambda i: (i, ob_col)),
                  pl.BlockSpec((tm, d), lambda i: (i, 0)),
                  _const_spec((d, d), 1), _const_spec((1, d), 1)],
        out_specs=pl.BlockSpec((tm, d), lambda i: (i, 0)),
        out_shape=jax.ShapeDtypeStruct((rows, d), F32),
        compiler_params=pltpu.CompilerParams(dimension_semantics=("arbitrary",),
                                             vmem_limit_bytes=VMEM_LIMIT),
        name="merge",
    )(x2, oa, ob, g, wo, gpost)


def _decode_kernel(pt_ref, q_ref, kn_ref, vn_ref, lfn_ref, *rest, nblk, dq, d, bps, ppb):
    del pt_ref
    n_pg = bps * ppb
    k_refs, v_refs, l_refs = rest[0:n_pg], rest[n_pg:2 * n_pg], rest[2 * n_pg:3 * n_pg]
    (decb_ref, decnew_ref, tri_ref, o_ref,
     qbd_ref, m_ref, l_ref, acc_ref, carry_ref, pm_ref, pls_ref, po_ref, gate_ref) = rest[3 * n_pg:]
    j = pl.program_id(1)
    half = d // 2
    n_heads = half // HEAD_DIM
    hr = dq * n_heads

    def bcast(x, ref):
        return jnp.broadcast_to(x, ref.shape)

    @pl.when(j == 0)
    def _init():
        q4 = q_ref[0].astype(F32) * (1.0 / math.sqrt(HEAD_DIM))
        rows = [jnp.broadcast_to(q4[qq:qq + 1], (n_heads, d)) for qq in range(dq)]
        qrep = jnp.concatenate(rows + rows, axis=0)
        r = lax.broadcasted_iota(jnp.int32, (2 * hr, d), 0)
        c = lax.broadcasted_iota(jnp.int32, (2 * hr, d), 1)
        head = (r // hr) * n_heads + r % n_heads
        qbd_ref[...] = jnp.where(c // HEAD_DIM == head, qrep, 0.0).astype(BF16)
        m_ref[...] = jnp.full(m_ref.shape, NEG, F32)
        l_ref[...] = jnp.zeros_like(l_ref)
        acc_ref[...] = jnp.zeros_like(acc_ref)
        carry_ref[...] = jnp.zeros_like(carry_ref)
        pm_ref[...] = jnp.full(pm_ref.shape, NEG, F32)
        pls_ref[...] = jnp.zeros_like(pls_ref)
        gate_ref[...] = jnp.zeros_like(gate_ref)

    qf = qbd_ref[0:hr, 0:half]
    qm = qbd_ref[hr:, half:]
    lane = lax.broadcasted_iota(jnp.int32, (hr, LANES), 1)

    def fox_bias(cum):
        return jnp.concatenate([cum[0:n_heads]] * dq, axis=0)

    def partial_softmax(s, vt16):
        mb = jnp.max(s, axis=1, keepdims=True)
        p = jnp.exp(s - mb)
        return mb, jnp.sum(p, axis=1, keepdims=True), _dot_nt(p.astype(BF16), vt16)

    def fox_merge(parts):
        m_old = m_ref[:, 0:1]
        m_new = m_old
        for mb, _, _ in parts:
            m_new = jnp.maximum(m_new, mb)
        a = jnp.exp(m_old - m_new)
        l_new = a * l_ref[:, 0:1]
        acc = a * acc_ref[...]
        for mb, lb, ob in parts:
            b = jnp.exp(mb - m_new)
            l_new = l_new + b * lb
            acc = acc + b * ob
        l_ref[...] = bcast(l_new, l_ref)
        acc_ref[...] = acc
        m_ref[...] = bcast(m_new, m_ref)

    run = carry_ref[:, 0:1]
    gate_new, pm_new, pls_new = gate_ref[...], pm_ref[...], pls_ref[...]
    fox_parts = []
    for u in range(bps):
        jb = j * bps + u
        kt16 = jnp.concatenate([r[0] for r in k_refs[u * ppb:(u + 1) * ppb]], axis=1).astype(BF16)
        vt16 = jnp.concatenate([r[0] for r in v_refs[u * ppb:(u + 1) * ppb]], axis=1).astype(BF16)

        lf = jnp.concatenate([r[0] for r in l_refs[u * ppb:(u + 1) * ppb]], axis=1)
        hi, mid, lo = _split3(jnp.concatenate([lf, jnp.zeros_like(lf)], axis=0))
        tri = tri_ref[...]
        cum = run + (_dot(hi, tri) + _dot(mid, tri) + _dot(lo, tri))
        run = cum[:, MOBA_BLOCK - 1:MOBA_BLOCK]

        s_f = _dot(qf, kt16[0:half]) - fox_bias(cum)
        s_raw = _dot(qm, kt16[half:])
        gate_new = jnp.where(lane == jb, jnp.sum(s_raw, axis=1, keepdims=True), gate_new)
        s_m = s_raw + decb_ref[jnp.where(jb == nblk - 1, 1, 0)][hr:]
        fox_parts.append(partial_softmax(s_f, vt16[0:half]))
        mb, lb, ob = partial_softmax(s_m, vt16[half:])
        pm_new = jnp.where(lane == jb, mb, pm_new)
        pls_new = jnp.where(lane == jb, lb, pls_new)
        po_ref[jb] = ob
    carry_ref[...] = bcast(run, carry_ref)
    gate_ref[...] = gate_new
    pm_ref[...] = pm_new
    pls_ref[...] = pls_new
    fox_merge(fox_parts)

    @pl.when(j == nblk // bps - 1)
    def _finish():
        pad = jnp.zeros((LANES - kn_ref.shape[1], d), F32)
        kn = jnp.concatenate([kn_ref[0], pad], axis=0).astype(BF16)
        vn = jnp.concatenate([vn_ref[0], pad], axis=0).astype(BF16)
        dn = decnew_ref[...]

        def new_softmax(s, v16):
            mb = jnp.max(s, axis=1, keepdims=True)
            p = jnp.exp(s - mb)
            return mb, jnp.sum(p, axis=1, keepdims=True), _dot(p.astype(BF16), v16)

        def heads_to_cols(o):
            r = lax.broadcasted_iota(jnp.int32, o.shape, 0)
            c = lax.broadcasted_iota(jnp.int32, o.shape, 1)
            o = jnp.where(c // HEAD_DIM == r % n_heads, o, 0.0)
            return jnp.concatenate([jnp.sum(o[qq * n_heads:(qq + 1) * n_heads], axis=0, keepdims=True)
                                    for qq in range(dq)], axis=0)

        lfn = lfn_ref[0]
        hi, mid, lo = _split3(jnp.concatenate([lfn, jnp.zeros_like(lfn)], axis=0))
        tri = tri_ref[0:LANES, 0:LANES]
        cn = carry_ref[:, 0:1] + (_dot(hi, tri) + _dot(mid, tri) + _dot(lo, tri))
        s_n = _dot_nt(qf, kn[:, 0:half]) - fox_bias(cn) + dn[0:hr]
        fox_merge([new_softmax(s_n, vn[:, 0:half])])
        o_fox = heads_to_cols(acc_ref[...] / l_ref[:, 0:1])

        gate = jnp.where(lane < nblk, gate_ref[...], jnp.where(lane == nblk, NEG, PICKED))
        sel = _top_blocks(gate, lane, nblk, axis=1)
        m_n, l_n, o_n = new_softmax(_dot_nt(qm, kn[:, half:]) + dn[hr:], vn[:, half:])
        pm_sel = jnp.where(sel, pm_ref[...], NEG)
        m_tot = jnp.maximum(jnp.max(pm_sel, axis=1, keepdims=True), m_n)
        w = jnp.exp(pm_sel - m_tot)
        w_n = jnp.exp(m_n - m_tot)
        l_tot = jnp.sum(w * pls_ref[...], axis=1, keepdims=True) + w_n * l_n
        o_tot = w_n * o_n
        for jj in range(nblk):
            o_tot = o_tot + w[:, jj:jj + 1] * po_ref[jj]
        o_moba = heads_to_cols(o_tot / l_tot)

        o_ref[0] = jnp.concatenate([o_fox, o_moba], axis=1)


def _decode(page_table, q3, kn3, vn3, lfn3, ckt, cvt, clt, decb, decnew, tri, *, page):
    n, dq, d = q3.shape
    n_pages = page_table.shape[1]
    ppb = MOBA_BLOCK // page
    nblk = n_pages // ppb
    bps = DECODE_BLOCKS_PER_STEP if nblk % DECODE_BLOCKS_PER_STEP == 0 else 1
    n_pg = bps * ppb
    half = d // 2
    hr = dq * (half // HEAD_DIM)
    kv_spec = lambda o: pl.BlockSpec((1, d, page), lambda s, j, pt: (pt[s, n_pg * j + o], 0, 0))
    lf_spec = lambda o: pl.BlockSpec((1, 8, page), lambda s, j, pt: (pt[s, n_pg * j + o], 0, 0))
    seq_spec = lambda a: pl.BlockSpec((1,) + a.shape[1:], lambda s, j, pt: (s, 0, 0))
    const = lambda a: pl.BlockSpec(a.shape, lambda s, j, pt: (0,) * a.ndim)
    grid_spec = pltpu.PrefetchScalarGridSpec(
        num_scalar_prefetch=1,
        grid=(n, nblk // bps),
        in_specs=([seq_spec(q3), seq_spec(kn3), seq_spec(vn3), seq_spec(lfn3)]
                  + [kv_spec(o) for o in range(n_pg)] * 2 + [lf_spec(o) for o in range(n_pg)]
                  + [const(decb), const(decnew), const(tri)]),
        out_specs=pl.BlockSpec((1, dq, d), lambda s, j, pt: (s, 0, 0)),
        scratch_shapes=[pltpu.VMEM((2 * hr, d), BF16),
                        pltpu.VMEM((hr, LANES), F32),
                        pltpu.VMEM((hr, LANES), F32),
                        pltpu.VMEM((hr, half), F32),
                        pltpu.VMEM((16, LANES), F32),
                        pltpu.VMEM((hr, LANES), F32),
                        pltpu.VMEM((hr, LANES), F32),
                        pltpu.VMEM((nblk, hr, half), F32),
                        pltpu.VMEM((hr, LANES), F32)])
    return pl.pallas_call(
        functools.partial(_decode_kernel, nblk=nblk, dq=dq, d=d, bps=bps, ppb=ppb),
        grid_spec=grid_spec,
        out_shape=jax.ShapeDtypeStruct((n, dq, d), F32),
        compiler_params=pltpu.CompilerParams(dimension_semantics=("arbitrary", "arbitrary"),
                                             vmem_limit_bytes=VMEM_LIMIT),
        name="decode",
    )(page_table, q3, kn3, vn3, lfn3, *([ckt] * n_pg), *([cvt] * n_pg), *([clt] * n_pg), decb, decnew, tri)


def _tri(n, lower):
    r = lax.broadcasted_iota(jnp.int32, (n, n), 0)
    c = lax.broadcasted_iota(jnp.int32, (n, n), 1)
    return ((r >= c) if lower else (r <= c)).astype(BF16)


def kernel(x_prompt, x_sample, cache_k, cache_v, cache_logf, page_table, g_pre, w_in, b_f, rel_bias, w_out, g_post):
    batch, seq, d = x_prompt.shape
    n_dec, dq, _ = x_sample.shape
    depth, n_pool, page = cache_k.shape[0], cache_k.shape[1], cache_k.shape[2]
    n_heads = d // HEAD_DIM
    fox_heads = n_heads // 2
    assert d == 2 * fox_heads * HEAD_DIM and fox_heads == 8 and rel_bias.shape == (REL_BUCKETS, fox_heads)
    assert seq % 512 == 0 and MOBA_BLOCK % page == 0 and (page_table.shape[1] * page) % MOBA_BLOCK == 0
    assert dq <= 8 and (n_dec * dq) % 8 == 0 and page_table.shape[1] * page // MOBA_BLOCK < LANES

    tm = 512
    tm_s = n_dec * dq if n_dec * dq <= 512 else 512
    assert (n_dec * dq) % tm_s == 0
    tri_p, tri_s, tri_d = _tri(tm, True), _tri(tm_s, True), _tri(MOBA_BLOCK, False)
    tbl = _prompt_bias(rel_bias)
    decb, decnew = _decode_bias(rel_bias, dq)

    xp = x_prompt.reshape(batch * seq, d)
    xs = x_sample.reshape(n_dec * dq, d)
    kp_l, vp_l, lp_l, ks_l, vs_l, ls_l = [], [], [], [], [], []
    for l in range(depth):
        wq = w_in[l][:, :4 * d].astype(BF16)
        wkt = w_in[l][:, d:2 * d].T.astype(BF16)
        wvt = w_in[l][:, 2 * d:3 * d].T.astype(BF16)
        wf = jnp.pad(w_in[l][:, 4 * d:], ((0, 0), (0, LANES - fox_heads))).astype(BF16)
        bfr = jnp.pad(b_f[l], (0, LANES - fox_heads))[None, :].astype(F32)
        gpre = g_pre[l][None, :]
        gpost = g_post[l][None, :]
        wo = w_out[l].astype(BF16)

        q, k, vt, g, kt32, vt32, lf, cum, kbar = _proj(xp, gpre, wq, wkt, wvt, wf, bfr, tri_p, tm=tm,
                                                       tiles_per_seq=seq // tm, kv_transposed=True)
        o_fox = _fox(q, k, vt, cum, batch=batch, seq=seq, tq=MOBA_BLOCK)
        o_moba = _moba(rel_bias, q, k, vt, kbar, tbl, batch=batch, seq=seq)
        kp_l.append(kt32.reshape(batch, n_heads, HEAD_DIM, seq).transpose(0, 3, 1, 2))
        vp_l.append(vt32.reshape(batch, n_heads, HEAD_DIM, seq).transpose(0, 3, 1, 2))
        lp_l.append(lf[:, :fox_heads].reshape(batch, seq, fox_heads))

        qs, _, _, gs, k32s, v32s, lfs, _, _ = _proj(xs, gpre, wq, wkt, wvt, wf, bfr, tri_s, tm=tm_s,
                                                    tiles_per_seq=1, kv_transposed=False)
        lfs = lfs[:, :fox_heads]
        pad_rows = lambda a: jnp.pad(a.reshape(n_dec, dq, d), ((0, 0), (0, 8 - dq), (0, 0)))
        lfn3 = jnp.pad(lfs.reshape(n_dec, dq, fox_heads).transpose(0, 2, 1), ((0, 0), (0, 0), (0, LANES - dq)))
        o_s = _decode(page_table, qs.reshape(n_dec, dq, d), pad_rows(k32s), pad_rows(v32s), lfn3,
                      cache_k[l].transpose(0, 2, 3, 1).reshape(n_pool, d, page),
                      cache_v[l].transpose(0, 2, 3, 1).reshape(n_pool, d, page),
                      cache_logf[l].transpose(0, 2, 1), decb, decnew, tri_d, page=page)
        o_s = o_s.reshape(n_dec * dq, d)
        ks_l.append(k32s.reshape(n_dec, dq, n_heads, HEAD_DIM))
        vs_l.append(v32s.reshape(n_dec, dq, n_heads, HEAD_DIM))
        ls_l.append(lfs.reshape(n_dec, dq, fox_heads))

        xp = _merge(xp, o_fox, o_moba, 0, g, wo, gpost, tm=tm)
        xs = _merge(xs, o_s, o_s, 1, gs, wo, gpost, tm=tm_s)

    return (xp.reshape(batch, seq, d), xs.reshape(n_dec, dq, d), jnp.stack(kp_l), jnp.stack(vp_l),
            jnp.stack(lp_l), jnp.stack(ks_l), jnp.stack(vs_l), jnp.stack(ls_l))
```
